```python
import jax, jax.numpy as jnp
from jax import lax
import numpy as np

D_MODEL = 2048
BATCH = 2
SEQ = 16384
DEPTH = 4
DEC_BATCH = 8
DEC_SEQ = 16
PAST_LEN = 1024

CHUNK = 64
D_MIX = D_MODEL
D_RG = D_MIX // 2
RG_BLOCKS = 8
RG_BW = D_RG // RG_BLOCKS
CONV_W = 4
LRU_C = 8.0
D_ML = D_MIX - D_RG
ML_HEADS = 4
ML_DH = D_ML // ML_HEADS
OFF_XR = 0
OFF_GR = OFF_XR + D_RG
OFF_Q = OFF_GR + D_RG
OFF_K = OFF_Q + D_ML
OFF_V = OFF_K + D_ML
OFF_O = OFF_V + D_ML
OFF_I = OFF_O + D_ML
OFF_F = OFF_I + ML_HEADS
D_IN = OFF_F + ML_HEADS
N_EXPERTS = 16
N_GROUPS = 4
EXPERTS_PER_GROUP = N_EXPERTS // N_GROUPS
TOP_K = 2
D_FF = D_MODEL // 2
MOE_BLOCK = 512
ALPHA = (2 * DEPTH) ** 0.25
BETA = (8 * DEPTH) ** -0.25
LN_EPS = 1e-5
RMS_EPS = 1e-6

kernel_name = 'hymba_rglru_mlstm_deepnorm_grouped_moe_step'


def layer_norm(x, g, b):
    xf = x.astype(jnp.float32)
    mu = jnp.mean(xf, axis=-1, keepdims=True)
    xc = xf - mu
    var = jnp.mean(xc * xc, axis=-1, keepdims=True)
    y = xc * lax.rsqrt(var + LN_EPS) * g.astype(jnp.float32) + b.astype(jnp.float32)
    return y.astype(x.dtype)


def causal_conv(u, buf, w, bias):
    T = u.shape[1]
    up = jnp.concatenate([buf.astype(u.dtype), u], axis=1)
    y = bias
    for tap in range(CONV_W):
        y = y + up[:, tap:tap + T] * w[tap]
    return y, up[:, up.shape[1] - (CONV_W - 1):]


def _linear_combine(e1, e2):
    a1, b1 = e1
    a2, b2 = e2
    return a1 * a2, a2 * b1 + b2


def rg_lru(xc, h0, wa, ba, wx, bx, lam):
    B, T, _ = xc.shape
    xh = xc.astype(jnp.float32).reshape(B, T, RG_BLOCKS, RG_BW)
    r = jax.nn.sigmoid(jnp.einsum('btnc,ncd->btnd', xh, wa.astype(jnp.float32))
                       + ba.astype(jnp.float32).reshape(RG_BLOCKS, RG_BW))
    ig = jax.nn.sigmoid(jnp.einsum('btnc,ncd->btnd', xh, wx.astype(jnp.float32))
                        + bx.astype(jnp.float32).reshape(RG_BLOCKS, RG_BW))
    log_a = LRU_C * r * jax.nn.log_sigmoid(lam.astype(jnp.float32)).reshape(RG_BLOCKS, RG_BW)
    a = jnp.exp(log_a)
    u = jnp.sqrt(-jnp.expm1(2.0 * log_a)) * (ig * xh)
    u = u.at[:, 0].add(a[:, 0] * h0.astype(jnp.float32).reshape(B, RG_BLOCKS, RG_BW))
    _, h = lax.associative_scan(_linear_combine, (a, u), axis=1)
    return h.reshape(B, T, D_RG), h[:, -1].reshape(B, D_RG)


def mlstm_chunk(carry, inp):
    C, n, m = carry
    q, k, v, ig, lf = inp
    L = q.shape[2]
    b = jnp.cumsum(lf, axis=-1)
    causal = jnp.tril(jnp.ones((L, L), dtype=bool))
    logw = jnp.where(causal, b[..., :, None] - b[..., None, :] + ig[..., None, :], -jnp.inf)
    g = b + m[..., None]
    mt = jnp.maximum(g, jnp.max(logw, axis=-1))
    w_intra = jnp.exp(logw - mt[..., None])
    w_state = jnp.exp(g - mt)
    s = jnp.einsum('bhtd,bhsd->bhts', q, k) * w_intra
    num = (w_state[..., None] * jnp.einsum('bhtd,bhde->bhte', q, C)
           + jnp.einsum('bhts,bhse->bhte', s, v))
    den = w_state * jnp.einsum('bhtd,bhd->bht', q, n) + jnp.sum(s, axis=-1)
    h = num / jnp.maximum(jnp.abs(den), jnp.exp(-mt))[..., None]
    b_last = b[..., -1]
    logu = b_last[..., None] - b + ig
    m_new = jnp.maximum(b_last + m, jnp.max(logu, axis=-1))
    u = jnp.exp(logu - m_new[..., None])
    decay = jnp.exp(b_last + m - m_new)
    C_new = decay[..., None, None] * C + jnp.einsum('bhs,bhsd,bhse->bhde', u, k, v)
    n_new = decay[..., None] * n + jnp.einsum('bhs,bhsd->bhd', u, k)
    return (C_new, n_new, m_new), h


def mlstm(q, k, v, ig_pre, f_pre, C0, n0, m0):
    B, T, _ = q.shape
    L = min(CHUNK, T)
    NC = T // L

    def heads(t):
        return t.astype(jnp.float32).reshape(B, NC, L, ML_HEADS, ML_DH).transpose(1, 0, 3, 2, 4)

    def gates(t):
        return t.astype(jnp.float32).reshape(B, NC, L, ML_HEADS).transpose(1, 0, 3, 2)

    qh = heads(q)
    kh = heads(k) * (ML_DH ** -0.5)
    vh = heads(v)
    ih = gates(ig_pre)
    lf = jax.nn.log_sigmoid(gates(f_pre))
    carry0 = (C0.astype(jnp.float32), n0.astype(jnp.float32), m0.astype(jnp.float32))
    (C, n, m), h = lax.scan(mlstm_chunk, carry0, (qh, kh, vh, ih, lf))
    h = h.transpose(1, 0, 3, 2, 4).reshape(B, T, ML_HEADS, ML_DH)
    return h, C, n, m


def mixer(x, conv_buf, h0, C0, n0, m0, w_in, b_in, conv_w, conv_b,
          wa, ba, wx, bx, lam, norm_g, w_out):
    B, T, _ = x.shape
    z = x @ w_in + b_in
    xr = z[..., OFF_XR:OFF_GR]
    gr = z[..., OFF_GR:OFF_Q]
    q = z[..., OFF_Q:OFF_K]
    k = z[..., OFF_K:OFF_V]
    v = z[..., OFF_V:OFF_O]
    o = z[..., OFF_O:OFF_I]
    ig = z[..., OFF_I:OFF_F]
    fg = z[..., OFF_F:D_IN]
    xc, new_buf = causal_conv(xr, conv_buf, conv_w, conv_b)
    hr, h_last = rg_lru(xc, h0, wa, ba, wx, bx, lam)
    rg_out = jax.nn.gelu(gr.astype(jnp.float32)) * hr
    hm, C, n, m = mlstm(q, k, v, ig, fg, C0, n0, m0)
    hm = hm * lax.rsqrt(jnp.mean(hm * hm, axis=-1, keepdims=True) + RMS_EPS)
    hm = hm * norm_g.astype(jnp.float32).reshape(ML_HEADS, ML_DH)
    ml_out = hm.reshape(B, T, D_ML) * jax.nn.sigmoid(o.astype(jnp.float32))
    mix = jnp.concatenate([rg_out, ml_out], axis=-1).astype(x.dtype)
    return mix @ w_out, (new_buf, h_last, C, n, m)


def moe_ffn(x, router_w, router_b, w1, w3, w2):
    B, T, D = x.shape
    N = B * T
    xt = x.reshape(N, D)
    logits = xt.astype(jnp.float32) @ router_w.astype(jnp.float32)
    probs = jax.nn.softmax(logits, axis=-1)
    sel = probs + router_b.astype(jnp.float32)
    sel_g = sel.reshape(N, N_GROUPS, EXPERTS_PER_GROUP)
    group_score = jnp.sum(lax.top_k(sel_g, 2)[0], axis=-1)
    g_idx = jnp.argmax(group_score, axis=-1).astype(jnp.int32)
    in_group = jnp.take_along_axis(sel_g, g_idx[:, None, None], axis=1)[:, 0]
    _, loc = lax.top_k(in_group, TOP_K)
    e_idx = g_idx[:, None] * EXPERTS_PER_GROUP + loc.astype(jnp.int32)
    gate = jnp.take_along_axis(probs, e_idx, axis=1)
    gate = gate / jnp.sum(gate, axis=-1, keepdims=True)
    P = N * TOP_K
    blk = min(MOE_BLOCK, max(8, 1 << max(0, P // N_EXPERTS - 1).bit_length()))
    nb = -(-(P + N_EXPERTS * (blk - 1)) // blk)
    R = nb * blk
    flat_e = e_idx.reshape(P)
    flat_tok = jnp.repeat(jnp.arange(N, dtype=jnp.int32), TOP_K)
    flat_g = gate.reshape(P)
    order = jnp.argsort(flat_e)
    se = flat_e[order]
    counts = jnp.bincount(flat_e, length=N_EXPERTS)
    start = jnp.cumsum(counts) - counts
    pcounts = (counts + blk - 1) // blk * blk
    pend = jnp.cumsum(pcounts)
    pstart = pend - pcounts
    dest = pstart[se] + (jnp.arange(P, dtype=jnp.int32) - start[se])
    buf_tok = jnp.full((R,), N, dtype=jnp.int32).at[dest].set(flat_tok[order])
    buf_g = jnp.zeros((R,), jnp.float32).at[dest].set(flat_g[order])
    blk_e = jnp.minimum(jnp.sum((jnp.arange(nb, dtype=jnp.int32) * blk)[:, None] >= pend[None, :], axis=1),
                        N_EXPERTS - 1)
    x_pad = jnp.concatenate([xt, jnp.zeros((1, D), xt.dtype)], axis=0)

    def expert_block(args):
        tok, g, e = args
        xb = x_pad[tok]
        hb = jax.nn.silu(xb @ w1[e]) * (xb @ w3[e])
        return (hb @ w2[e]) * g[:, None]

    yb = lax.map(expert_block, (buf_tok.reshape(nb, blk), buf_g.astype(x.dtype).reshape(nb, blk), blk_e))
    out = jnp.zeros((N + 1, D), x.dtype).at[buf_tok].add(yb.reshape(R, D))[:N]
    return out.reshape(B, T, D)


def trunk(x, conv0, lru0, C0, n0, m0, w):
    x = layer_norm(x, w['ln_in_g'], w['ln_in_b'])
    convs, lrus, Cs, ns, ms = [], [], [], [], []
    for l in range(DEPTH):
        y, (cb, hl, C, n, m) = mixer(
            x, conv0[l], lru0[l], C0[l], n0[l], m0[l],
            w['w_in'][l], w['b_in'][l], w['conv_w'][l], w['conv_b'][l],
            w['lru_wa'][l], w['lru_ba'][l], w['lru_wx'][l], w['lru_bx'][l], w['lru_lambda'][l],
            w['mlstm_norm_g'][l], w['w_out'][l])
        x = layer_norm(ALPHA * x + y, w['ln1_g'][l], w['ln1_b'][l])
        f = moe_ffn(x, w['router_w'], w['router_b'], w['w1'][l], w['w3'][l], w['w2'][l])
        x = layer_norm(ALPHA * x + f, w['ln2_g'][l], w['ln2_b'][l])
        convs.append(cb)
        lrus.append(hl)
        Cs.append(C)
        ns.append(n)
        ms.append(m)
    return x, jnp.stack(convs), jnp.stack(lrus), jnp.stack(Cs), jnp.stack(ns), jnp.stack(ms)


def setup_inputs(seed: int = 0) -> dict:
    key = jax.random.key(seed)
    ks = jax.random.split(key, 32)
    nrm = jax.random.normal
    f32 = jnp.float32
    b_in = 0.01 * nrm(ks[8], (DEPTH, D_IN), f32)
    f_bias = jnp.linspace(3.0, 6.0, ML_HEADS, dtype=f32) + 0.01 * nrm(ks[9], (DEPTH, ML_HEADS), f32)
    b_in = b_in.at[:, OFF_F:D_IN].set(f_bias)
    a0 = jax.random.uniform(ks[15], (DEPTH, D_RG), f32, minval=0.9, maxval=0.999)
    return {
        'x_prompt': nrm(ks[0], (BATCH, SEQ, D_MODEL), f32),
        'x_sample': nrm(ks[1], (DEC_BATCH, DEC_SEQ, D_MODEL), f32),
        'state_conv': nrm(ks[2], (DEPTH, DEC_BATCH, CONV_W - 1, D_RG), f32),
        'state_lru': 0.5 * nrm(ks[3], (DEPTH, DEC_BATCH, D_RG), f32),
        'state_mlstm_C': 0.05 * nrm(ks[4], (DEPTH, DEC_BATCH, ML_HEADS, ML_DH, ML_DH), f32),
        'state_mlstm_n': 0.05 * nrm(ks[5], (DEPTH, DEC_BATCH, ML_HEADS, ML_DH), f32),
        'state_mlstm_m': nrm(ks[6], (DEPTH, DEC_BATCH, ML_HEADS), f32),
        'ln_in_g': 1.0 + 0.01 * nrm(ks[7], (D_MODEL,), f32),
        'ln_in_b': 0.01 * nrm(ks[10], (D_MODEL,), f32),
        'w_in': nrm(ks[11], (DEPTH, D_MODEL, D_IN), f32) * D_MODEL ** -0.5,
        'b_in': b_in,
        'conv_w': nrm(ks[12], (DEPTH, CONV_W, D_RG), f32) * CONV_W ** -0.5,
        'conv_b': 0.01 * nrm(ks[13], (DEPTH, D_RG), f32),
        'lru_wa': nrm(ks[14], (DEPTH, RG_BLOCKS, RG_BW, RG_BW), f32) * RG_BW ** -0.5,
        'lru_ba': 0.01 * nrm(ks[16], (DEPTH, D_RG), f32),
        'lru_wx': nrm(ks[17], (DEPTH, RG_BLOCKS, RG_BW, RG_BW), f32) * RG_BW ** -0.5,
        'lru_bx': 0.01 * nrm(ks[18], (DEPTH, D_RG), f32),
        'lru_lambda': jnp.log(a0) - jnp.log1p(-a0),
        'mlstm_norm_g': 1.0 + 0.01 * nrm(ks[19], (DEPTH, D_ML), f32),
        'w_out': nrm(ks[20], (DEPTH, D_MIX, D_MODEL), f32) * (D_MIX ** -0.5) * BETA,
        'ln1_g': 1.0 + 0.01 * nrm(ks[21], (DEPTH, D_MODEL), f32),
        'ln1_b': 0.01 * nrm(ks[22], (DEPTH, D_MODEL), f32),
        'router_w': nrm(ks[23], (D_MODEL, N_EXPERTS), f32) * D_MODEL ** -0.5,
        'router_b': 0.01 * nrm(ks[24], (N_EXPERTS,), f32),
        'w1': nrm(ks[25], (DEPTH, N_EXPERTS, D_MODEL, D_FF), f32) * D_MODEL ** -0.5,
        'w3': nrm(ks[26], (DEPTH, N_EXPERTS, D_MODEL, D_FF), f32) * D_MODEL ** -0.5,
        'w2': nrm(ks[27], (DEPTH, N_EXPERTS, D_FF, D_MODEL), f32) * (D_FF ** -0.5) * BETA,
        'ln2_g': 1.0 + 0.01 * nrm(ks[28], (DEPTH, D_MODEL), f32),
        'ln2_b': 0.01 * nrm(ks[29], (DEPTH, D_MODEL), f32),
    }


def reference(x_prompt, x_sample, state_conv, state_lru, state_mlstm_C, state_mlstm_n, state_mlstm_m,
              ln_in_g, ln_in_b, w_in, b_in, conv_w, conv_b, lru_wa, lru_ba, lru_wx, lru_bx, lru_lambda,
              mlstm_norm_g, w_out, ln1_g, ln1_b, router_w, router_b, w1, w3, w2, ln2_g, ln2_b):
    w = dict(ln_in_g=ln_in_g, ln_in_b=ln_in_b, w_in=w_in, b_in=b_in, conv_w=conv_w, conv_b=conv_b,
             lru_wa=lru_wa, lru_ba=lru_ba, lru_wx=lru_wx, lru_bx=lru_bx, lru_lambda=lru_lambda,
             mlstm_norm_g=mlstm_norm_g, w_out=w_out, ln1_g=ln1_g, ln1_b=ln1_b,
             router_w=router_w, router_b=router_b, w1=w1, w3=w3, w2=w2, ln2_g=ln2_g, ln2_b=ln2_b)
    B = x_prompt.shape[0]
    z_conv = jnp.zeros((DEPTH, B, CONV_W - 1, D_RG), x_prompt.dtype)
    z_lru = jnp.zeros((DEPTH, B, D_RG), jnp.float32)
    z_C = jnp.zeros((DEPTH, B, ML_HEADS, ML_DH, ML_DH), jnp.float32)
    z_n = jnp.zeros((DEPTH, B, ML_HEADS, ML_DH), jnp.float32)
    z_m = jnp.zeros((DEPTH, B, ML_HEADS), jnp.float32)
    y_prompt, p_conv, p_lru, p_C, p_n, p_m = trunk(x_prompt, z_conv, z_lru, z_C, z_n, z_m, w)
    y_sample, s_conv, s_lru, s_C, s_n, s_m = trunk(
        x_sample, state_conv, state_lru, state_mlstm_C, state_mlstm_n, state_mlstm_m, w)
    return (y_prompt, y_sample, p_conv, p_lru, p_C, p_n, p_m, s_conv, s_lru, s_C, s_n, s_m)
```

```python
import functools

import jax
import jax.numpy as jnp
from jax import lax
from jax.experimental import pallas as pl
from jax.experimental.pallas import tpu as pltpu

DEPTH = 4
D_MODEL = 2048
D_RG = 1024
RG_BLOCKS = 8
RG_BW = D_RG // RG_BLOCKS
CONV_W = 4
LRU_C = 8.0
D_ML = 1024
ML_HEADS = 4
ML_DH = D_ML // ML_HEADS
N_EXPERTS = 16
N_GROUPS = 4
EXPERTS_PER_GROUP = N_EXPERTS // N_GROUPS
TOP_K = 2
D_FF = D_MODEL // 2
ALPHA = (2 * DEPTH) ** 0.25
LN_EPS = 1e-5
RMS_EPS = 1e-6
N_PROJ = 6
GATE_OFF = N_PROJ * D_RG

LANES = 128
SUBLANES = 8
MXU_DTYPE = jnp.bfloat16
SLOT = 128
TM_PROJ = 1024
TM_LN = 512
TM_OUT = 256
TM_CMB = 256
TB_MOE = 256
T_RG = 512
L_ML = 256
VMEM_LIMIT = 56 * 1024 * 1024


def _cparams(sem):
    return pltpu.CompilerParams(dimension_semantics=sem, vmem_limit_bytes=VMEM_LIMIT)


def _layer_norm(h, g, b):
    mu = jnp.mean(h, axis=-1, keepdims=True)
    hc = h - mu
    var = jnp.mean(hc * hc, axis=-1, keepdims=True)
    return hc * lax.rsqrt(var + LN_EPS) * g + b


def _ln_kernel(x_ref, g_ref, b_ref, o_ref, ob_ref):
    y = _layer_norm(x_ref[...], g_ref[...], b_ref[...])
    o_ref[...] = y
    ob_ref[...] = y.astype(ob_ref.dtype)


def _ln_call(x, g, b):
    n, d = x.shape
    row = pl.BlockSpec((TM_LN, d), lambda i: (i, 0))
    vec = pl.BlockSpec((1, d), lambda i: (0, 0))
    return pl.pallas_call(
        _ln_kernel,
        grid=(n // TM_LN,),
        in_specs=[row, vec, vec],
        out_specs=[row, row],
        out_shape=[jax.ShapeDtypeStruct((n, d), jnp.float32),
                   jax.ShapeDtypeStruct((n, d), MXU_DTYPE)],
        compiler_params=_cparams(("parallel",)),
        name="ln_in",
    )(x, g.reshape(1, d), b.reshape(1, d))


def _proj_kernel(x_ref, w_ref, b_ref, o_ref):
    o_ref[...] = jnp.dot(x_ref[...], w_ref[...], preferred_element_type=jnp.float32) + b_ref[...]


def _in_proj_call(xb, w, b):
    n, d = xb.shape
    return pl.pallas_call(
        _proj_kernel,
        grid=(N_PROJ, n // TM_PROJ),
        in_specs=[pl.BlockSpec((TM_PROJ, d), lambda j, i: (i, 0)),
                  pl.BlockSpec((d, D_RG), lambda j, i: (0, j)),
                  pl.BlockSpec((1, D_RG), lambda j, i: (0, j))],
        out_specs=pl.BlockSpec((None, TM_PROJ, D_RG), lambda j, i: (j, i, 0)),
        out_shape=jax.ShapeDtypeStruct((N_PROJ, n, D_RG), jnp.float32),
        compiler_params=_cparams(("parallel", "parallel")),
        name="in_proj",
    )(xb, w, b)


def _gate_proj_call(xb, wg, bg):
    n, d = xb.shape
    return pl.pallas_call(
        _proj_kernel,
        grid=(n // TM_PROJ,),
        in_specs=[pl.BlockSpec((TM_PROJ, d), lambda i: (i, 0)),
                  pl.BlockSpec((d, LANES), lambda i: (0, 0)),
                  pl.BlockSpec((1, LANES), lambda i: (0, 0))],
        out_specs=pl.BlockSpec((TM_PROJ, LANES), lambda i: (i, 0)),
        out_shape=jax.ShapeDtypeStruct((n, LANES), jnp.float32),
        compiler_params=_cparams(("parallel",)),
        name="gate_proj",
    )(xb, wg, bg)


def _rglru_kernel(xr_ref, gr_ref, cs_ref, h0_ref, cw_ref, cb_ref, wg_ref, ba_ref, bx_ref, lam_ref,
                  mix_ref, cso_ref, ho_ref, ext_ref, a_ref, u_ref, hc_ref, *, tv, n_t):
    t = pl.program_id(1)
    tt = mix_ref.shape[0]

    @pl.when(t == 0)
    def _():
        ext_ref[0:SUBLANES, :] = jnp.zeros((SUBLANES, D_RG), jnp.float32)
        ext_ref[SUBLANES - (CONV_W - 1):SUBLANES, :] = cs_ref[0]
        hc_ref[...] = h0_ref[0]

    u_in = xr_ref[0:tv, :]
    ext_ref[SUBLANES:SUBLANES + tv, :] = u_in
    cw = cw_ref[...]
    xc = cb_ref[...]
    for tap in range(CONV_W - 1):
        off = SUBLANES - (CONV_W - 1) + tap
        xc = xc + ext_ref[off:off + tv, :] * cw[tap:tap + 1, :]
    xc = xc + u_in * cw[CONV_W - 1:CONV_W, :]

    @pl.when(t == n_t - 1)
    def _():
        cso_ref[0] = ext_ref[SUBLANES + tv - (CONV_W - 1):SUBLANES + tv, :]

    ext_ref[0:SUBLANES, :] = ext_ref[tv:tv + SUBLANES, :]

    xcb = xc.astype(MXU_DTYPE)
    rowmod = lax.broadcasted_iota(jnp.int32, (tv, RG_BW), 0) % SUBLANES
    for nb in range(RG_BLOCKS):
        cs = slice(nb * RG_BW, (nb + 1) * RG_BW)
        pre = jnp.dot(xcb[:, cs], wg_ref[nb], preferred_element_type=jnp.float32)
        r = jax.nn.sigmoid(pre[:, :RG_BW] + ba_ref[:, cs])
        ig = jax.nn.sigmoid(pre[:, RG_BW:] + bx_ref[:, cs])
        log_a = LRU_C * r * jax.nn.log_sigmoid(lam_ref[:, cs])
        a = jnp.exp(log_a)
        u = jnp.sqrt(1.0 - a * a) * (ig * xc[:, cs])
        for s in (1, 2, 4):
            a_s = pltpu.roll(a, s, 0)
            u_s = pltpu.roll(u, s, 0)
            m = rowmod >= s
            u = jnp.where(m, a * u_s + u, u)
            a = jnp.where(m, a * a_s, a)
        a_ref[0:tv, cs] = a
        u_ref[0:tv, cs] = u

    def body(g, carry):
        rows = pl.ds(pl.multiple_of(g * SUBLANES, SUBLANES), SUBLANES)
        h = a_ref[rows, :] * carry + u_ref[rows, :]
        u_ref[rows, :] = h
        return h[SUBLANES - 1:SUBLANES, :]

    h_last = lax.fori_loop(0, tv // SUBLANES, body, hc_ref[...])
    hc_ref[...] = h_last
    ho_ref[0] = h_last

    mix_ref[0:tv, :] = (jax.nn.gelu(gr_ref[0:tv, :]) * u_ref[0:tv, :]).astype(mix_ref.dtype)
    if tv < tt:
        mix_ref[tv:tt, :] = jnp.zeros((tt - tv, D_RG), mix_ref.dtype)


def _rglru_call(z3, conv0, h0, cw, cb, wg, ba, bx, lam, *, row0, seq_stride, tt, tv, n_t, name):
    n_seq = conv0.shape[0]
    blk0, sb = row0 // tt, seq_stride // tt

    def rows(b, t):
        return blk0 + b * sb + t

    vec = pl.BlockSpec((1, D_RG), lambda b, t: (0, 0))
    in_specs = [
        pl.BlockSpec((None, tt, D_RG), lambda b, t: (0, rows(b, t), 0)),
        pl.BlockSpec((None, tt, D_RG), lambda b, t: (1, rows(b, t), 0)),
        pl.BlockSpec((1, CONV_W - 1, D_RG), lambda b, t: (b, 0, 0)),
        pl.BlockSpec((1, 1, D_RG), lambda b, t: (b, 0, 0)),
        pl.BlockSpec((CONV_W, D_RG), lambda b, t: (0, 0)),
        vec,
        pl.BlockSpec((RG_BLOCKS, RG_BW, 2 * RG_BW), lambda b, t: (0, 0, 0)),
        vec, vec, vec,
    ]
    args = [z3, z3, conv0, h0.reshape(n_seq, 1, D_RG), cw, cb.reshape(1, D_RG), wg,
            ba.reshape(1, D_RG), bx.reshape(1, D_RG), lam.reshape(1, D_RG)]
    mix, conv_out, h_out = pl.pallas_call(
        functools.partial(_rglru_kernel, tv=tv, n_t=n_t),
        grid=(n_seq, n_t),
        in_specs=in_specs,
        out_specs=[pl.BlockSpec((tt, D_RG), lambda b, t: (b * sb + t, 0)),
                   pl.BlockSpec((1, CONV_W - 1, D_RG), lambda b, t: (b, 0, 0)),
                   pl.BlockSpec((1, 1, D_RG), lambda b, t: (b, 0, 0))],
        out_shape=[jax.ShapeDtypeStruct((n_seq * seq_stride, D_RG), MXU_DTYPE),
                   jax.ShapeDtypeStruct((n_seq, CONV_W - 1, D_RG), jnp.float32),
                   jax.ShapeDtypeStruct((n_seq, 1, D_RG), jnp.float32)],
        scratch_shapes=[pltpu.VMEM((tt + SUBLANES, D_RG), jnp.float32),
                        pltpu.VMEM((tt, D_RG), jnp.float32),
                        pltpu.VMEM((tt, D_RG), jnp.float32),
                        pltpu.VMEM((1, D_RG), jnp.float32)],
        compiler_params=_cparams(("parallel", "arbitrary")),
        name=name,
    )(*args)
    return mix, conv_out, h_out.reshape(n_seq, D_RG)


def _mlstm_kernel(q_ref, k_ref, v_ref, o_ref, g_ref, c0_ref, n0_ref, m0_ref, ng_ref,
                  mix_ref, co_ref, no_ref, mo_ref, c_s, n_s, m_s, *, tv, n_c):
    c = pl.program_id(1)
    ll = mix_ref.shape[0]
    h_n = ML_HEADS

    @pl.when(c == 0)
    def _():
        c_s[...] = c0_ref[0]
        n_s[...] = n0_ref[0]
        m_s[...] = m0_ref[0]

    g = g_ref[...]
    row = lax.broadcasted_iota(jnp.int32, (ll, LANES), 0)
    lf = jax.nn.log_sigmoid(g)
    if tv < ll:
        lane = lax.broadcasted_iota(jnp.int32, (ll, LANES), 1)
        g = jnp.where((row >= tv) & (lane < h_n), -jnp.inf, g)
        lf = jnp.where(row >= tv, 0.0, lf)
    bcum = lf
    s = 1
    while s < ll:
        bcum = bcum + jnp.where(row >= s, pltpu.roll(bcum, s, 0), 0.0)
        s *= 2
    g_t = g.T
    b_t = bcum.T
    ti = lax.broadcasted_iota(jnp.int32, (ll, ll), 0)
    si = lax.broadcasted_iota(jnp.int32, (ll, ll), 1)
    causal = si <= ti
    m_all = m_s[...]
    dn_t = (((1,), (1,)), ((), ()))
    dn_c0 = (((0,), (0,)), ((), ()))

    for h in range(h_n):
        cs = slice(h * ML_DH, (h + 1) * ML_DH)
        qf = q_ref[:, cs]
        kf = k_ref[:, cs] * (ML_DH ** -0.5)
        vf = v_ref[:, cs]
        qb, kb, vb = qf.astype(MXU_DTYPE), kf.astype(MXU_DTYPE), vf.astype(MXU_DTYPE)
        ig_col = g[:, h:h + 1]
        b_col = bcum[:, h_n + h:h_n + h + 1]
        ig_row = g_t[h:h + 1, :]
        b_row = b_t[h_n + h:h_n + h + 1, :]
        m_prev = m_all[:, h:h + 1]

        logw = jnp.where(causal, b_col - b_row + ig_row, -jnp.inf)
        gq = b_col + m_prev
        mt = jnp.maximum(gq, jnp.max(logw, axis=-1, keepdims=True))
        w_intra = jnp.exp(logw - mt)
        w_state = jnp.exp(gq - mt)
        s_mat = lax.dot_general(qb, kb, dn_t, preferred_element_type=jnp.float32) * w_intra
        c_prev = c_s[h]
        n_prev = n_s[h:h + 1, :]
        num = (w_state * jnp.dot(qb, c_prev.astype(MXU_DTYPE), preferred_element_type=jnp.float32)
               + jnp.dot(s_mat.astype(MXU_DTYPE), vb, preferred_element_type=jnp.float32))
        den = w_state * jnp.sum(qf * n_prev, axis=-1, keepdims=True) + jnp.sum(s_mat, axis=-1, keepdims=True)
        hh = num / jnp.maximum(jnp.abs(den), jnp.exp(-mt))

        b_last = b_col[ll - 1:ll, :]
        logu_row = b_last - b_row + ig_row
        m_new = jnp.maximum(b_last + m_prev, jnp.max(logu_row, axis=-1, keepdims=True))
        u_col = jnp.exp(b_last - b_col + ig_col - m_new)
        decay = jnp.exp(b_last + m_prev - m_new)
        ku = kf * u_col
        c_s[h] = decay * c_prev + lax.dot_general(ku.astype(MXU_DTYPE), vb, dn_c0,
                                                  preferred_element_type=jnp.float32)
        n_s[h:h + 1, :] = decay * n_prev + jnp.sum(ku, axis=0, keepdims=True)
        m_s[:, h:h + 1] = m_new

        hm = hh * lax.rsqrt(jnp.mean(hh * hh, axis=-1, keepdims=True) + RMS_EPS) * ng_ref[:, cs]
        mix_ref[:, cs] = (hm * jax.nn.sigmoid(o_ref[:, cs])).astype(mix_ref.dtype)

    @pl.when(c == n_c - 1)
    def _():
        co_ref[0] = c_s[...]
        no_ref[0] = n_s[...]
        mo_ref[0] = m_s[...]


def _mlstm_call(z3, gates, c0, n0, m0, ng, *, row0, seq_stride, ll, tv, n_c, name):
    n_seq = c0.shape[0]
    blk0, sb = row0 // ll, seq_stride // ll

    def rows(b, c):
        return blk0 + b * sb + c

    def zspec(j):
        return pl.BlockSpec((None, ll, D_ML), lambda b, c: (j, rows(b, c), 0))

    in_specs = [
        zspec(2), zspec(3), zspec(4), zspec(5),
        pl.BlockSpec((ll, LANES), lambda b, c: (rows(b, c), 0)),
        pl.BlockSpec((1, ML_HEADS, ML_DH, ML_DH), lambda b, c: (b, 0, 0, 0)),
        pl.BlockSpec((1, ML_HEADS, ML_DH), lambda b, c: (b, 0, 0)),
        pl.BlockSpec((1, 1, ML_HEADS), lambda b, c: (b, 0, 0)),
        pl.BlockSpec((1, D_ML), lambda b, c: (0, 0)),
    ]
    args = [z3, z3, z3, z3, gates, c0, n0, m0.reshape(n_seq, 1, ML_HEADS), ng.reshape(1, D_ML)]
    mix, c_out, n_out, m_out = pl.pallas_call(
        functools.partial(_mlstm_kernel, tv=tv, n_c=n_c),
        grid=(n_seq, n_c),
        in_specs=in_specs,
        out_specs=[pl.BlockSpec((ll, D_ML), lambda b, c: (b * sb + c, 0)),
                   pl.BlockSpec((1, ML_HEADS, ML_DH, ML_DH), lambda b, c: (b, 0, 0, 0)),
                   pl.BlockSpec((1, ML_HEADS, ML_DH), lambda b, c: (b, 0, 0)),
                   pl.BlockSpec((1, 1, ML_HEADS), lambda b, c: (b, 0, 0))],
        out_shape=[jax.ShapeDtypeStruct((n_seq * seq_stride, D_ML), MXU_DTYPE),
                   jax.ShapeDtypeStruct((n_seq, ML_HEADS, ML_DH, ML_DH), jnp.float32),
                   jax.ShapeDtypeStruct((n_seq, ML_HEADS, ML_DH), jnp.float32),
                   jax.ShapeDtypeStruct((n_seq, 1, ML_HEADS), jnp.float32)],
        scratch_shapes=[pltpu.VMEM((ML_HEADS, ML_DH, ML_DH), jnp.float32),
                        pltpu.VMEM((ML_HEADS, ML_DH), jnp.float32),
                        pltpu.VMEM((1, ML_HEADS), jnp.float32)],
        compiler_params=_cparams(("parallel", "arbitrary")),
        name=name,
    )(*args)
    return mix, c_out, n_out, m_out.reshape(n_seq, ML_HEADS)


def _first_argmax(vals):
    best, idx = vals[0], jnp.zeros(vals[0].shape, jnp.int32)
    for j in range(1, len(vals)):
        better = vals[j] > best
        best = jnp.where(better, vals[j], best)
        idx = jnp.where(better, j, idx)
    return best, idx


def _pick(vals, idx):
    out = vals[0]
    for j in range(1, len(vals)):
        out = jnp.where(idx == j, vals[j], out)
    return out


def _outproj_kernel(x_ref, rgp_ref, mlp_ref, rgs_ref, mls_ref, wo_ref, g_ref, b_ref, rwt_ref, rb_ref,
                    x1_ref, x1b_ref, e_ref, gc_ref, *, n_prompt_tiles):
    tm = x_ref.shape[0]
    is_prompt = pl.program_id(0) < n_prompt_tiles
    mix_rg = jnp.where(is_prompt, rgp_ref[...], rgs_ref[...])
    mix_ml = jnp.where(is_prompt, mlp_ref[...], mls_ref[...])
    y = (jnp.dot(mix_rg, wo_ref[0:D_RG, :], preferred_element_type=jnp.float32)
         + jnp.dot(mix_ml, wo_ref[D_RG:, :], preferred_element_type=jnp.float32))
    x1 = _layer_norm(ALPHA * x_ref[...] + y, g_ref[...], b_ref[...])
    x1_ref[...] = x1
    x1b = x1.astype(x1b_ref.dtype)
    x1b_ref[...] = x1b

    logits = lax.dot_general(rwt_ref[...], x1b, (((1,), (1,)), ((), ())),
                             preferred_element_type=jnp.float32)
    ex = jnp.exp(logits - jnp.max(logits, axis=0, keepdims=True))
    probs = ex / jnp.sum(ex, axis=0, keepdims=True)
    sel = probs + rb_ref[...]
    sel_r = [sel[e:e + 1, :] for e in range(N_EXPERTS)]
    prob_r = [probs[e:e + 1, :] for e in range(N_EXPERTS)]
    scores = []
    for gi in range(N_GROUPS):
        v = sel_r[gi * EXPERTS_PER_GROUP:(gi + 1) * EXPERTS_PER_GROUP]
        top2 = None
        for i in range(EXPERTS_PER_GROUP):
            for j in range(i + 1, EXPERTS_PER_GROUP):
                pair = v[i] + v[j]
                top2 = pair if top2 is None else jnp.maximum(top2, pair)
        scores.append(top2)
    _, g_idx = _first_argmax(scores)
    in_sel = [_pick([sel_r[gi * EXPERTS_PER_GROUP + j] for gi in range(N_GROUPS)], g_idx)
              for j in range(EXPERTS_PER_GROUP)]
    in_prob = [_pick([prob_r[gi * EXPERTS_PER_GROUP + j] for gi in range(N_GROUPS)], g_idx)
               for j in range(EXPERTS_PER_GROUP)]
    _, loc0 = _first_argmax(in_sel)
    masked = [jnp.where(loc0 == j, -jnp.inf, in_sel[j]) for j in range(EXPERTS_PER_GROUP)]
    _, loc1 = _first_argmax(masked)
    p0, p1 = _pick(in_prob, loc0), _pick(in_prob, loc1)
    psum = p0 + p1
    e0 = g_idx * EXPERTS_PER_GROUP + loc0
    e1 = g_idx * EXPERTS_PER_GROUP + loc1
    e_ref[...] = jnp.concatenate([e0, e1, jnp.zeros((SUBLANES - TOP_K, tm), jnp.int32)], axis=0)
    gates_t = jnp.concatenate([p0 / psum, p1 / psum, jnp.zeros((LANES - TOP_K, tm), jnp.float32)], axis=0)
    gc_ref[...] = gates_t.T


def _outproj_call(x, rg_p, ml_p, rg_s, ml_s, wo, g, b, rwt, rb):
    n, d = x.shape
    tm = TM_OUT
    npt = rg_p.shape[0] // tm
    row = pl.BlockSpec((tm, d), lambda i: (i, 0))
    half_p = pl.BlockSpec((tm, D_RG), lambda i: (jnp.minimum(i, npt - 1), 0))
    half_s = pl.BlockSpec((tm, D_RG), lambda i: (jnp.maximum(i - npt, 0), 0))
    vec = pl.BlockSpec((1, d), lambda i: (0, 0))
    return pl.pallas_call(
        functools.partial(_outproj_kernel, n_prompt_tiles=npt),
        grid=(n // tm,),
        in_specs=[row, half_p, half_p, half_s, half_s,
                  pl.BlockSpec((d, d), lambda i: (0, 0)),
                  vec, vec,
                  pl.BlockSpec((N_EXPERTS, d), lambda i: (0, 0)),
                  pl.BlockSpec((N_EXPERTS, 1), lambda i: (0, 0))],
        out_specs=[row, row,
                   pl.BlockSpec((SUBLANES, tm), lambda i: (0, i)),
                   pl.BlockSpec((tm, LANES), lambda i: (i, 0))],
        out_shape=[jax.ShapeDtypeStruct((n, d), jnp.float32),
                   jax.ShapeDtypeStruct((n, d), MXU_DTYPE),
                   jax.ShapeDtypeStruct((SUBLANES, n), jnp.int32),
                   jax.ShapeDtypeStruct((n, LANES), jnp.float32)],
        compiler_params=_cparams(("parallel",)),
        name="outproj_ln_router",
    )(x, rg_p, ml_p, rg_s, ml_s, wo, g.reshape(1, d), b.reshape(1, d), rwt, rb.reshape(N_EXPERTS, 1))


def _dispatch_plan(e_t, valid, tb, nb):
    n = e_t.shape[1]
    p = TOP_K * n
    flat_e = jnp.where(valid[:, None], e_t.T, N_EXPERTS).reshape(p)
    order = jnp.argsort(flat_e, stable=True)
    se = flat_e[order]
    counts = jnp.bincount(flat_e, length=N_EXPERTS + 1)[:N_EXPERTS]
    start = jnp.cumsum(counts) - counts
    pcounts = (counts + tb - 1) // tb * tb
    pend = jnp.cumsum(pcounts)
    pstart = pend - pcounts
    sec = jnp.minimum(se, N_EXPERTS - 1)
    dest_sorted = pstart[sec] + (jnp.arange(p, dtype=jnp.int32) - start[sec])
    dest_sorted = jnp.where(se < N_EXPERTS, dest_sorted, nb * tb)
    buf_tok = jnp.zeros((nb * tb,), jnp.int32).at[dest_sorted].set((order // TOP_K).astype(jnp.int32), mode="drop")
    dest = jnp.zeros((p,), jnp.int32).at[order].set(jnp.where(se < N_EXPERTS, dest_sorted, 0).astype(jnp.int32))
    blk_e = jnp.minimum(jnp.sum((jnp.arange(nb, dtype=jnp.int32) * tb)[:, None] >= pend[None, :], axis=1),
                        N_EXPERTS - 1).astype(jnp.int32)
    n_used = (pend[-1] // tb).astype(jnp.int32).reshape(1)
    return buf_tok, dest.reshape(n, TOP_K).T, blk_e, n_used


def _moe_kernel(be_ref, nu_ref, tok_ref, tokn_ref, x_hbm, w1_ref, w3_ref, w2_ref, y_ref, xbuf, sem):
    r = pl.program_id(0)
    tb = y_ref.shape[0]
    slot = r % 2
    n_used = nu_ref[0]

    def gather(idx_ref, dst_slot):
        def issue(j, carry):
            pltpu.make_async_copy(x_hbm.at[pl.ds(idx_ref[j], 1), :],
                                  xbuf.at[dst_slot, pl.ds(j, 1), :], sem.at[dst_slot]).start()
            return carry
        lax.fori_loop(0, tb, issue, 0)

    @pl.when(r == 0)
    def _():
        gather(tok_ref, 0)

    @pl.when(r + 1 < n_used)
    def _():
        gather(tokn_ref, 1 - slot)

    @pl.when(r < n_used)
    def _():
        pltpu.make_async_copy(x_hbm.at[pl.ds(0, tb), :], xbuf.at[slot], sem.at[slot]).wait()
        xb = xbuf[slot].astype(MXU_DTYPE)
        h1 = jnp.dot(xb, w1_ref[0], preferred_element_type=jnp.float32)
        h3 = jnp.dot(xb, w3_ref[0], preferred_element_type=jnp.float32)
        hb = (jax.nn.silu(h1) * h3).astype(MXU_DTYPE)
        y_ref[...] = jnp.dot(hb, w2_ref[0], preferred_element_type=jnp.float32)

    @pl.when(r >= n_used)
    def _():
        y_ref[...] = jnp.zeros(y_ref.shape, y_ref.dtype)


def _moe_call(x1, buf_tok, blk_e, n_used, w1, w3, w2, tb, nb):
    n, d = x1.shape
    f = w1.shape[-1]

    def used(r, be, nu):
        return jnp.minimum(r, nu[0] - 1)

    grid_spec = pltpu.PrefetchScalarGridSpec(
        num_scalar_prefetch=2,
        grid=(nb,),
        in_specs=[
            pl.BlockSpec((tb,), lambda r, be, nu: (used(r, be, nu),), memory_space=pltpu.SMEM),
            pl.BlockSpec((tb,), lambda r, be, nu: (jnp.minimum(r + 1, nu[0] - 1),), memory_space=pltpu.SMEM),
            pl.BlockSpec(memory_space=pl.ANY),
            pl.BlockSpec((1, d, f), lambda r, be, nu: (be[used(r, be, nu)], 0, 0)),
            pl.BlockSpec((1, d, f), lambda r, be, nu: (be[used(r, be, nu)], 0, 0)),
            pl.BlockSpec((1, f, d), lambda r, be, nu: (be[used(r, be, nu)], 0, 0)),
        ],
        out_specs=pl.BlockSpec((tb, d), lambda r, be, nu: (r, 0)),
        scratch_shapes=[pltpu.VMEM((2, tb, d), jnp.float32), pltpu.SemaphoreType.DMA((2,))],
    )
    return pl.pallas_call(
        _moe_kernel,
        grid_spec=grid_spec,
        out_shape=jax.ShapeDtypeStruct((nb * tb, d), jnp.float32),
        compiler_params=_cparams(("arbitrary",)),
        name="moe_experts",
    )(blk_e, n_used, buf_tok, buf_tok, x1, w1, w3, w2)


def _combine_kernel(d0_ref, d1_ref, d0n_ref, d1n_ref, x_ref, gc_ref, g_ref, b_ref, y_hbm,
                    o_ref, ob_ref, ybuf, sem):
    i = pl.program_id(0)
    n_i = pl.num_programs(0)
    tm = x_ref.shape[0]
    slot = i % 2

    def gather(r0, r1, dst_slot):
        def issue(j, carry):
            pltpu.make_async_copy(y_hbm.at[pl.ds(r0[j], 1), :],
                                  ybuf.at[dst_slot, pl.ds(j, 1), :], sem.at[dst_slot]).start()
            pltpu.make_async_copy(y_hbm.at[pl.ds(r1[j], 1), :],
                                  ybuf.at[dst_slot, pl.ds(tm + j, 1), :], sem.at[dst_slot]).start()
            return carry
        lax.fori_loop(0, tm, issue, 0)

    @pl.when(i == 0)
    def _():
        gather(d0_ref, d1_ref, 0)

    @pl.when(i + 1 < n_i)
    def _():
        gather(d0n_ref, d1n_ref, 1 - slot)

    pltpu.make_async_copy(y_hbm.at[pl.ds(0, TOP_K * tm), :], ybuf.at[slot], sem.at[slot]).wait()
    gc = gc_ref[...]
    f = ybuf[slot, 0:tm, :] * gc[:, 0:1] + ybuf[slot, tm:TOP_K * tm, :] * gc[:, 1:2]
    y = _layer_norm(ALPHA * x_ref[...] + f, g_ref[...], b_ref[...])
    o_ref[...] = y
    ob_ref[...] = y.astype(ob_ref.dtype)


def _combine_call(x1, gcol, dest, ys, g, b):
    n, d = x1.shape
    tm = TM_CMB
    n_i = n // tm
    row = pl.BlockSpec((tm, d), lambda i: (i, 0))
    vec = pl.BlockSpec((1, d), lambda i: (0, 0))
    cur = pl.BlockSpec((tm,), lambda i: (i,), memory_space=pltpu.SMEM)
    nxt = pl.BlockSpec((tm,), lambda i: (jnp.minimum(i + 1, n_i - 1),), memory_space=pltpu.SMEM)
    return pl.pallas_call(
        _combine_kernel,
        grid=(n_i,),
        in_specs=[cur, cur, nxt, nxt, row,
                  pl.BlockSpec((tm, LANES), lambda i: (i, 0)),
                  vec, vec,
                  pl.BlockSpec(memory_space=pl.ANY)],
        out_specs=[row, row],
        out_shape=[jax.ShapeDtypeStruct((n, d), jnp.float32),
                   jax.ShapeDtypeStruct((n, d), MXU_DTYPE)],
        scratch_shapes=[pltpu.VMEM((2, TOP_K * tm, d), jnp.float32), pltpu.SemaphoreType.DMA((2,))],
        compiler_params=_cparams(("arbitrary",)),
        name="combine_ln",
    )(dest[0], dest[1], dest[0], dest[1], x1, gcol, g.reshape(1, d), b.reshape(1, d), ys)


def kernel(x_prompt, x_sample, state_conv, state_lru, state_mlstm_C, state_mlstm_n, state_mlstm_m,
           ln_in_g, ln_in_b, w_in, b_in, conv_w, conv_b, lru_wa, lru_ba, lru_wx, lru_bx, lru_lambda,
           mlstm_norm_g, w_out, ln1_g, ln1_b, router_w, router_b, w1, w3, w2, ln2_g, ln2_b):
    bp, tp, d = x_prompt.shape
    bs, ts, _ = x_sample.shape
    depth = w_in.shape[0]
    n_p = bp * tp
    n_s = bs * SLOT
    n_pad = n_p + n_s
    assert d == D_MODEL and ts <= SLOT and ts % SUBLANES == 0
    assert tp % T_RG == 0 and tp % L_ML == 0
    assert n_p % TM_PROJ == 0 and n_s % TM_PROJ == 0
    f32 = jnp.float32

    x_all = jnp.concatenate(
        [x_prompt.reshape(n_p, d),
         jnp.pad(x_sample, ((0, 0), (0, SLOT - ts), (0, 0))).reshape(n_s, d)], axis=0)
    row_id = jnp.arange(n_pad, dtype=jnp.int32)
    valid = (row_id < n_p) | (((row_id - n_p) % SLOT) < ts)

    nb = -(-(TOP_K * (n_p + bs * ts) + N_EXPERTS * (TB_MOE - 1)) // TB_MOE)

    w_in_m = w_in[:, :, :GATE_OFF].astype(MXU_DTYPE)
    w_gate_m = jnp.pad(w_in[:, :, GATE_OFF:], ((0, 0), (0, 0), (0, LANES - 2 * ML_HEADS))).astype(MXU_DTYPE)
    b_gate = jnp.pad(b_in[:, GATE_OFF:], ((0, 0), (0, LANES - 2 * ML_HEADS)))
    w_rg_m = jnp.concatenate([lru_wa, lru_wx], axis=-1).astype(MXU_DTYPE)
    w_out_m = w_out.astype(MXU_DTYPE)
    rwt_m = router_w.T.astype(MXU_DTYPE)
    w1_m, w3_m, w2_m = w1.astype(MXU_DTYPE), w3.astype(MXU_DTYPE), w2.astype(MXU_DTYPE)

    zp = lambda *s: jnp.zeros(s, f32)
    x, xb = _ln_call(x_all, ln_in_g, ln_in_b)
    outs = {k: [] for k in ("pc", "pl", "pC", "pn", "pm", "sc", "sl", "sC", "sn", "sm")}
    for l in range(depth):
        z3 = _in_proj_call(xb, w_in_m[l], b_in[l, :GATE_OFF].reshape(1, GATE_OFF))
        gates = _gate_proj_call(xb, w_gate_m[l], b_gate[l].reshape(1, LANES))

        rg_w = (conv_w[l], conv_b[l], w_rg_m[l], lru_ba[l], lru_bx[l], lru_lambda[l])
        rg_p, pc, plru = _rglru_call(z3, zp(bp, CONV_W - 1, D_RG), zp(bp, D_RG), *rg_w, row0=0, seq_stride=tp,
                                     tt=T_RG, tv=T_RG, n_t=tp // T_RG, name="rglru_prompt")
        rg_s, sc, slru = _rglru_call(z3, state_conv[l], state_lru[l], *rg_w, row0=n_p, seq_stride=SLOT,
                                     tt=SLOT, tv=ts, n_t=1, name="rglru_sample")
        ml_p, pC, pn, pm = _mlstm_call(z3, gates, zp(bp, ML_HEADS, ML_DH, ML_DH), zp(bp, ML_HEADS, ML_DH),
                                       zp(bp, ML_HEADS), mlstm_norm_g[l], row0=0, seq_stride=tp,
                                       ll=L_ML, tv=L_ML, n_c=tp // L_ML, name="mlstm_prompt")
        ml_s, sC, sn, sm = _mlstm_call(z3, gates, state_mlstm_C[l], state_mlstm_n[l], state_mlstm_m[l],
                                       mlstm_norm_g[l], row0=n_p, seq_stride=SLOT,
                                       ll=SLOT, tv=ts, n_c=1, name="mlstm_sample")

        x1, x1b, e_t, gcol = _outproj_call(x, rg_p, ml_p, rg_s, ml_s, w_out_m[l], ln1_g[l], ln1_b[l],
                                           rwt_m, router_b)
        buf_tok, dest, blk_e, n_used = _dispatch_plan(e_t[:TOP_K], valid, TB_MOE, nb)
        ys = _moe_call(x1, buf_tok, blk_e, n_used, w1_m[l], w3_m[l], w2_m[l], TB_MOE, nb)
        x, xb = _combine_call(x1, gcol, dest, ys, ln2_g[l], ln2_b[l])

        for k, v in zip(("pc", "pl", "pC", "pn", "pm", "sc", "sl", "sC", "sn", "sm"),
                        (pc, plru, pC, pn, pm, sc, slru, sC, sn, sm)):
            outs[k].append(v)

    y_prompt = x[:n_p].reshape(bp, tp, d)
    y_sample = x[n_p:].reshape(bs, SLOT, d)[:, :ts]
    st = {k: jnp.stack(v) for k, v in outs.items()}
    return (y_prompt, y_sample, st["pc"], st["pl"], st["pC"], st["pn"], st["pm"],
            st["sc"], st["sl"], st["sC"], st["sn"], st["sm"])
```

```python
import functools

import jax
import jax.numpy as jnp
from jax import lax
from jax.experimental import pallas as pl
from jax.experimental.pallas import tpu as pltpu

DEPTH = 4
D_MODEL = 2048
D_RG = 1024
RG_BLOCKS = 8
RG_BW = D_RG // RG_BLOCKS
CONV_W = 4
LRU_C = 8.0
D_ML = 1024
ML_HEADS = 4
ML_DH = D_ML // ML_HEADS
CHUNK = 64
N_EXPERTS = 16
N_GROUPS = 4
EXPERTS_PER_GROUP = N_EXPERTS // N_GROUPS
TOP_K = 2
D_FF = D_MODEL // 2
ALPHA = (2 * DEPTH) ** 0.25
LN_EPS = 1e-5
RMS_EPS = 1e-6
N_PROJ = 6
GATE_OFF = N_PROJ * D_RG

LANES = 128
SUBLANES = 8
MXU_DTYPE = jnp.bfloat16
SLOT = 128
TM_PROJ = 1024
TM_LN = 512
TM_OUT = 256
TM_CMB = 512
TB_MOE = 256
T_RG = 512
L_ML = 256
VMEM_LIMIT = 56 * 1024 * 1024


def _cparams(sem):
    return pltpu.CompilerParams(dimension_semantics=sem, vmem_limit_bytes=VMEM_LIMIT)


def _layer_norm(h, g, b):
    mu = jnp.mean(h, axis=-1, keepdims=True)
    hc = h - mu
    var = jnp.mean(hc * hc, axis=-1, keepdims=True)
    return hc * lax.rsqrt(var + LN_EPS) * g + b


def _ln_kernel(xp_ref, xs_ref, g_ref, b_ref, o_ref, ob_ref, *, n_prompt_tiles):
    x = jnp.where(pl.program_id(0) < n_prompt_tiles, xp_ref[...], xs_ref[...])
    y = _layer_norm(x, g_ref[...], b_ref[...])
    o_ref[...] = y
    ob_ref[...] = y.astype(ob_ref.dtype)


def _ln_call(xp, xs, g, b):
    (n_p, d), n_s = xp.shape, xs.shape[0]
    tm = TM_LN
    npt = n_p // tm
    n = n_p + n_s
    row = pl.BlockSpec((tm, d), lambda i: (i, 0))
    vec = pl.BlockSpec((1, d), lambda i: (0, 0))
    return pl.pallas_call(
        functools.partial(_ln_kernel, n_prompt_tiles=npt),
        grid=(n // tm,),
        in_specs=[pl.BlockSpec((tm, d), lambda i: (jnp.minimum(i, npt - 1), 0)),
                  pl.BlockSpec((tm, d), lambda i: (jnp.maximum(i - npt, 0), 0)),
                  vec, vec],
        out_specs=[row, row],
        out_shape=[jax.ShapeDtypeStruct((n, d), jnp.float32),
                   jax.ShapeDtypeStruct((n, d), MXU_DTYPE)],
        compiler_params=_cparams(("parallel",)),
        name="ln_in",
    )(xp, xs, g.reshape(1, d), b.reshape(1, d))


def _proj_kernel(x_ref, w_ref, b_ref, o_ref):
    o_ref[...] = jnp.dot(x_ref[...], w_ref[...], preferred_element_type=jnp.float32) + b_ref[...]


def _in_proj_call(xb, w, b):
    n, d = xb.shape
    return pl.pallas_call(
        _proj_kernel,
        grid=(N_PROJ, n // TM_PROJ),
        in_specs=[pl.BlockSpec((TM_PROJ, d), lambda j, i: (i, 0)),
                  pl.BlockSpec((d, D_RG), lambda j, i: (0, j)),
                  pl.BlockSpec((1, D_RG), lambda j, i: (0, j))],
        out_specs=pl.BlockSpec((None, TM_PROJ, D_RG), lambda j, i: (j, i, 0)),
        out_shape=jax.ShapeDtypeStruct((N_PROJ, n, D_RG), jnp.float32),
        compiler_params=_cparams(("parallel", "parallel")),
        name="in_proj",
    )(xb, w, b)


def _gate_proj_call(xb, wg, bg):
    n, d = xb.shape
    return pl.pallas_call(
        _proj_kernel,
        grid=(n // TM_PROJ,),
        in_specs=[pl.BlockSpec((TM_PROJ, d), lambda i: (i, 0)),
                  pl.BlockSpec((d, LANES), lambda i: (0, 0)),
                  pl.BlockSpec((1, LANES), lambda i: (0, 0))],
        out_specs=pl.BlockSpec((TM_PROJ, LANES), lambda i: (i, 0)),
        out_shape=jax.ShapeDtypeStruct((n, LANES), jnp.float32),
        compiler_params=_cparams(("parallel",)),
        name="gate_proj",
    )(xb, wg, bg)


def _rglru_kernel(xr_ref, gr_ref, cs_ref, h0_ref, cw_ref, cb_ref, wg_ref, ba_ref, bx_ref, lam_ref,
                  mix_ref, cso_ref, ho_ref, ext_ref, a_ref, u_ref, hc_ref, *, tv, n_t):
    t = pl.program_id(1)
    tt = mix_ref.shape[0]

    @pl.when(t == 0)
    def _():
        ext_ref[0:SUBLANES, :] = jnp.zeros((SUBLANES, D_RG), jnp.float32)
        ext_ref[SUBLANES - (CONV_W - 1):SUBLANES, :] = cs_ref[0]
        hc_ref[...] = h0_ref[0]

    u_in = xr_ref[0:tv, :]
    ext_ref[SUBLANES:SUBLANES + tv, :] = u_in
    cw = cw_ref[...]
    xc = cb_ref[...]
    for tap in range(CONV_W - 1):
        off = SUBLANES - (CONV_W - 1) + tap
        xc = xc + ext_ref[off:off + tv, :] * cw[tap:tap + 1, :]
    xc = xc + u_in * cw[CONV_W - 1:CONV_W, :]

    @pl.when(t == n_t - 1)
    def _():
        cso_ref[0] = ext_ref[SUBLANES + tv - (CONV_W - 1):SUBLANES + tv, :]

    ext_ref[0:SUBLANES, :] = ext_ref[tv:tv + SUBLANES, :]

    xcb = xc.astype(MXU_DTYPE)
    rowmod = lax.broadcasted_iota(jnp.int32, (tv, RG_BW), 0) % SUBLANES
    for nb in range(RG_BLOCKS):
        cs = slice(nb * RG_BW, (nb + 1) * RG_BW)
        pre = jnp.dot(xcb[:, cs], wg_ref[nb], preferred_element_type=jnp.float32)
        r = jax.nn.sigmoid(pre[:, :RG_BW] + ba_ref[:, cs])
        ig = jax.nn.sigmoid(pre[:, RG_BW:] + bx_ref[:, cs])
        log_a = LRU_C * r * jax.nn.log_sigmoid(lam_ref[:, cs])
        a = jnp.exp(log_a)
        th = jnp.tanh(log_a)
        u = jnp.sqrt(-2.0 * th / (1.0 - th)) * (ig * xc[:, cs])
        for s in (1, 2, 4):
            a_s = pltpu.roll(a, s, 0)
            u_s = pltpu.roll(u, s, 0)
            m = rowmod >= s
            u = jnp.where(m, a * u_s + u, u)
            a = jnp.where(m, a * a_s, a)
        a_ref[0:tv, cs] = a
        u_ref[0:tv, cs] = u

    def body(g, carry):
        rows = pl.ds(pl.multiple_of(g * SUBLANES, SUBLANES), SUBLANES)
        h = a_ref[rows, :] * carry + u_ref[rows, :]
        u_ref[rows, :] = h
        return h[SUBLANES - 1:SUBLANES, :]

    h_last = lax.fori_loop(0, tv // SUBLANES, body, hc_ref[...])
    hc_ref[...] = h_last
    ho_ref[0] = h_last

    mix_ref[0:tv, :] = (jax.nn.gelu(gr_ref[0:tv, :]) * u_ref[0:tv, :]).astype(mix_ref.dtype)
    if tv < tt:
        mix_ref[tv:tt, :] = jnp.zeros((tt - tv, D_RG), mix_ref.dtype)


def _rglru_call(z3, conv0, h0, cw, cb, wg, ba, bx, lam, *, row0, seq_stride, tt, tv, n_t, name):
    n_seq = conv0.shape[0]
    blk0, sb = row0 // tt, seq_stride // tt

    def rows(b, t):
        return blk0 + b * sb + t

    vec = pl.BlockSpec((1, D_RG), lambda b, t: (0, 0))
    in_specs = [
        pl.BlockSpec((None, tt, D_RG), lambda b, t: (0, rows(b, t), 0)),
        pl.BlockSpec((None, tt, D_RG), lambda b, t: (1, rows(b, t), 0)),
        pl.BlockSpec((1, CONV_W - 1, D_RG), lambda b, t: (b, 0, 0)),
        pl.BlockSpec((1, 1, D_RG), lambda b, t: (b, 0, 0)),
        pl.BlockSpec((CONV_W, D_RG), lambda b, t: (0, 0)),
        vec,
        pl.BlockSpec((RG_BLOCKS, RG_BW, 2 * RG_BW), lambda b, t: (0, 0, 0)),
        vec, vec, vec,
    ]
    args = [z3, z3, conv0, h0.reshape(n_seq, 1, D_RG), cw, cb.reshape(1, D_RG), wg,
            ba.reshape(1, D_RG), bx.reshape(1, D_RG), lam.reshape(1, D_RG)]
    mix, conv_out, h_out = pl.pallas_call(
        functools.partial(_rglru_kernel, tv=tv, n_t=n_t),
        grid=(n_seq, n_t),
        in_specs=in_specs,
        out_specs=[pl.BlockSpec((tt, D_RG), lambda b, t: (b * sb + t, 0)),
                   pl.BlockSpec((1, CONV_W - 1, D_RG), lambda b, t: (b, 0, 0)),
                   pl.BlockSpec((1, 1, D_RG), lambda b, t: (b, 0, 0))],
        out_shape=[jax.ShapeDtypeStruct((n_seq * seq_stride, D_RG), MXU_DTYPE),
                   jax.ShapeDtypeStruct((n_seq, CONV_W - 1, D_RG), jnp.float32),
                   jax.ShapeDtypeStruct((n_seq, 1, D_RG), jnp.float32)],
        scratch_shapes=[pltpu.VMEM((tt + SUBLANES, D_RG), jnp.float32),
                        pltpu.VMEM((tt, D_RG), jnp.float32),
                        pltpu.VMEM((tt, D_RG), jnp.float32),
                        pltpu.VMEM((1, D_RG), jnp.float32)],
        compiler_params=_cparams(("parallel", "arbitrary")),
        name=name,
    )(*args)
    return mix, conv_out, h_out.reshape(n_seq, D_RG)


def _mlstm_kernel(q_ref, k_ref, v_ref, o_ref, g_ref, c0_ref, n0_ref, m0_ref, ng_ref,
                  mix_ref, co_ref, no_ref, mo_ref, c_s, n_s, m_s, *, lc, tv, n_c):
    c = pl.program_id(1)
    ll = mix_ref.shape[0]
    h_n = ML_HEADS

    @pl.when(c == 0)
    def _():
        c_s[...] = c0_ref[0]
        n_s[...] = n0_ref[0]
        m_s[...] = m0_ref[0]

    g = g_ref[...]
    row = lax.broadcasted_iota(jnp.int32, (ll, LANES), 0)
    lf = jax.nn.log_sigmoid(g)
    if tv < ll:
        lane = lax.broadcasted_iota(jnp.int32, (ll, LANES), 1)
        g = jnp.where((row >= tv) & (lane < h_n), -jnp.inf, g)
        lf = jnp.where(row >= tv, 0.0, lf)
    rowc = row % lc
    bcum = lf
    s = 1
    while s < lc:
        bcum = bcum + jnp.where(rowc >= s, pltpu.roll(bcum, s, 0), 0.0)
        s *= 2
    g_t = g.T
    b_t = bcum.T
    ti = lax.broadcasted_iota(jnp.int32, (lc, lc), 0)
    si = lax.broadcasted_iota(jnp.int32, (lc, lc), 1)
    causal = si <= ti
    dn_t = (((1,), (1,)), ((), ()))
    dn_c0 = (((0,), (0,)), ((), ()))

    for j in range(ll // lc):
        rs = slice(j * lc, (j + 1) * lc)
        for h in range(h_n):
            cs = slice(h * ML_DH, (h + 1) * ML_DH)
            qf = q_ref[rs, cs]
            kf = k_ref[rs, cs] * (ML_DH ** -0.5)
            vf = v_ref[rs, cs]
            qb, kb, vb = qf.astype(MXU_DTYPE), kf.astype(MXU_DTYPE), vf.astype(MXU_DTYPE)
            ig_col = g[rs, h:h + 1]
            b_col = bcum[rs, h_n + h:h_n + h + 1]
            ig_row = g_t[h:h + 1, rs]
            b_row = b_t[h_n + h:h_n + h + 1, rs]
            m_prev = m_s[:, h:h + 1]

            logw = jnp.where(causal, b_col - b_row + ig_row, -jnp.inf)
            gq = b_col + m_prev
            mt = jnp.maximum(gq, jnp.max(logw, axis=-1, keepdims=True))
            w_intra = jnp.exp(logw - mt)
            w_state = jnp.exp(gq - mt)
            s_mat = lax.dot_general(qb, kb, dn_t, preferred_element_type=jnp.float32) * w_intra
            c_prev = c_s[h]
            n_prev = n_s[h:h + 1, :]
            num = (w_state * jnp.dot(qb, c_prev.astype(MXU_DTYPE), preferred_element_type=jnp.float32)
                   + jnp.dot(s_mat.astype(MXU_DTYPE), vb, preferred_element_type=jnp.float32))
            den = (w_state * jnp.sum(qf * n_prev, axis=-1, keepdims=True)
                   + jnp.sum(s_mat, axis=-1, keepdims=True))
            hh = num / jnp.maximum(jnp.abs(den), jnp.exp(-mt))

            b_last = b_col[lc - 1:lc, :]
            logu_row = b_last - b_row + ig_row
            m_new = jnp.maximum(b_last + m_prev, jnp.max(logu_row, axis=-1, keepdims=True))
            u_col = jnp.exp(b_last - b_col + ig_col - m_new)
            decay = jnp.exp(b_last + m_prev - m_new)
            ku = kf * u_col
            c_s[h] = decay * c_prev + lax.dot_general(ku.astype(MXU_DTYPE), vb, dn_c0,
                                                      preferred_element_type=jnp.float32)
            n_s[h:h + 1, :] = decay * n_prev + jnp.sum(ku, axis=0, keepdims=True)
            m_s[:, h:h + 1] = m_new

            hm = hh * lax.rsqrt(jnp.mean(hh * hh, axis=-1, keepdims=True) + RMS_EPS) * ng_ref[:, cs]
            mix_ref[rs, cs] = (hm * jax.nn.sigmoid(o_ref[rs, cs])).astype(mix_ref.dtype)

    @pl.when(c == n_c - 1)
    def _():
        co_ref[0] = c_s[...]
        no_ref[0] = n_s[...]
        mo_ref[0] = m_s[...]


def _mlstm_call(z3, gates, c0, n0, m0, ng, *, row0, seq_stride, ll, lc, tv, n_c, name):
    n_seq = c0.shape[0]
    blk0, sb = row0 // ll, seq_stride // ll

    def rows(b, c):
        return blk0 + b * sb + c

    def zspec(j):
        return pl.BlockSpec((None, ll, D_ML), lambda b, c: (j, rows(b, c), 0))

    in_specs = [
        zspec(2), zspec(3), zspec(4), zspec(5),
        pl.BlockSpec((ll, LANES), lambda b, c: (rows(b, c), 0)),
        pl.BlockSpec((1, ML_HEADS, ML_DH, ML_DH), lambda b, c: (b, 0, 0, 0)),
        pl.BlockSpec((1, ML_HEADS, ML_DH), lambda b, c: (b, 0, 0)),
        pl.BlockSpec((1, 1, ML_HEADS), lambda b, c: (b, 0, 0)),
        pl.BlockSpec((1, D_ML), lambda b, c: (0, 0)),
    ]
    args = [z3, z3, z3, z3, gates, c0, n0, m0.reshape(n_seq, 1, ML_HEADS), ng.reshape(1, D_ML)]
    mix, c_out, n_out, m_out = pl.pallas_call(
        functools.partial(_mlstm_kernel, lc=lc, tv=tv, n_c=n_c),
        grid=(n_seq, n_c),
        in_specs=in_specs,
        out_specs=[pl.BlockSpec((ll, D_ML), lambda b, c: (b * sb + c, 0)),
                   pl.BlockSpec((1, ML_HEADS, ML_DH, ML_DH), lambda b, c: (b, 0, 0, 0)),
                   pl.BlockSpec((1, ML_HEADS, ML_DH), lambda b, c: (b, 0, 0)),
                   pl.BlockSpec((1, 1, ML_HEADS), lambda b, c: (b, 0, 0))],
        out_shape=[jax.ShapeDtypeStruct((n_seq * seq_stride, D_ML), MXU_DTYPE),
                   jax.ShapeDtypeStruct((n_seq, ML_HEADS, ML_DH, ML_DH), jnp.float32),
                   jax.ShapeDtypeStruct((n_seq, ML_HEADS, ML_DH), jnp.float32),
                   jax.ShapeDtypeStruct((n_seq, 1, ML_HEADS), jnp.float32)],
        scratch_shapes=[pltpu.VMEM((ML_HEADS, ML_DH, ML_DH), jnp.float32),
                        pltpu.VMEM((ML_HEADS, ML_DH), jnp.float32),
                        pltpu.VMEM((1, ML_HEADS), jnp.float32)],
        compiler_params=_cparams(("parallel", "arbitrary")),
        name=name,
    )(*args)
    return mix, c_out, n_out, m_out.reshape(n_seq, ML_HEADS)


def _first_argmax(vals):
    best, idx = vals[0], jnp.zeros(vals[0].shape, jnp.int32)
    for j in range(1, len(vals)):
        better = vals[j] > best
        best = jnp.where(better, vals[j], best)
        idx = jnp.where(better, j, idx)
    return best, idx


def _pick(vals, idx):
    out = vals[0]
    for j in range(1, len(vals)):
        out = jnp.where(idx == j, vals[j], out)
    return out


def _outproj_kernel(x_ref, rgp_ref, mlp_ref, rgs_ref, mls_ref, wo_ref, g_ref, b_ref, rwt_ref, rb_ref,
                    x1_ref, x1b_ref, e_ref, gc_ref, *, n_prompt_tiles):
    tm = x_ref.shape[0]
    is_prompt = pl.program_id(0) < n_prompt_tiles
    mix_rg = jnp.where(is_prompt, rgp_ref[...], rgs_ref[...])
    mix_ml = jnp.where(is_prompt, mlp_ref[...], mls_ref[...])
    y = (jnp.dot(mix_rg, wo_ref[0:D_RG, :], preferred_element_type=jnp.float32)
         + jnp.dot(mix_ml, wo_ref[D_RG:, :], preferred_element_type=jnp.float32))
    x1 = _layer_norm(ALPHA * x_ref[...] + y, g_ref[...], b_ref[...])
    x1_ref[...] = x1
    x1b = x1.astype(x1b_ref.dtype)
    x1b_ref[...] = x1b

    logits = lax.dot_general(rwt_ref[...], x1b, (((1,), (1,)), ((), ())),
                             preferred_element_type=jnp.float32)
    ex = jnp.exp(logits - jnp.max(logits, axis=0, keepdims=True))
    probs = ex / jnp.sum(ex, axis=0, keepdims=True)
    sel = probs + rb_ref[...]
    sel_r = [sel[e:e + 1, :] for e in range(N_EXPERTS)]
    prob_r = [probs[e:e + 1, :] for e in range(N_EXPERTS)]
    scores = []
    for gi in range(N_GROUPS):
        v = sel_r[gi * EXPERTS_PER_GROUP:(gi + 1) * EXPERTS_PER_GROUP]
        top2 = None
        for i in range(EXPERTS_PER_GROUP):
            for j in range(i + 1, EXPERTS_PER_GROUP):
                pair = v[i] + v[j]
                top2 = pair if top2 is None else jnp.maximum(top2, pair)
        scores.append(top2)
    _, g_idx = _first_argmax(scores)
    in_sel = [_pick([sel_r[gi * EXPERTS_PER_GROUP + j] for gi in range(N_GROUPS)], g_idx)
              for j in range(EXPERTS_PER_GROUP)]
    in_prob = [_pick([prob_r[gi * EXPERTS_PER_GROUP + j] for gi in range(N_GROUPS)], g_idx)
               for j in range(EXPERTS_PER_GROUP)]
    _, loc0 = _first_argmax(in_sel)
    masked = [jnp.where(loc0 == j, -jnp.inf, in_sel[j]) for j in range(EXPERTS_PER_GROUP)]
    _, loc1 = _first_argmax(masked)
    p0, p1 = _pick(in_prob, loc0), _pick(in_prob, loc1)
    psum = p0 + p1
    e0 = g_idx * EXPERTS_PER_GROUP + loc0
    e1 = g_idx * EXPERTS_PER_GROUP + loc1
    e_ref[...] = jnp.concatenate([e0, e1, jnp.zeros((SUBLANES - TOP_K, tm), jnp.int32)], axis=0)
    gates_t = jnp.concatenate([p0 / psum, p1 / psum, jnp.zeros((LANES - TOP_K, tm), jnp.float32)], axis=0)
    gc_ref[...] = gates_t.T


def _outproj_call(x, rg_p, ml_p, rg_s, ml_s, wo, g, b, rwt, rb):
    n, d = x.shape
    tm = TM_OUT
    npt = rg_p.shape[0] // tm
    row = pl.BlockSpec((tm, d), lambda i: (i, 0))
    half_p = pl.BlockSpec((tm, D_RG), lambda i: (jnp.minimum(i, npt - 1), 0))
    half_s = pl.BlockSpec((tm, D_RG), lambda i: (jnp.maximum(i - npt, 0), 0))
    vec = pl.BlockSpec((1, d), lambda i: (0, 0))
    return pl.pallas_call(
        functools.partial(_outproj_kernel, n_prompt_tiles=npt),
        grid=(n // tm,),
        in_specs=[row, half_p, half_p, half_s, half_s,
                  pl.BlockSpec((d, d), lambda i: (0, 0)),
                  vec, vec,
                  pl.BlockSpec((N_EXPERTS, d), lambda i: (0, 0)),
                  pl.BlockSpec((N_EXPERTS, 1), lambda i: (0, 0))],
        out_specs=[row, row,
                   pl.BlockSpec((SUBLANES, tm), lambda i: (0, i)),
                   pl.BlockSpec((tm, LANES), lambda i: (i, 0))],
        out_shape=[jax.ShapeDtypeStruct((n, d), jnp.float32),
                   jax.ShapeDtypeStruct((n, d), MXU_DTYPE),
                   jax.ShapeDtypeStruct((SUBLANES, n), jnp.int32),
                   jax.ShapeDtypeStruct((n, LANES), jnp.float32)],
        compiler_params=_cparams(("parallel",)),
        name="outproj_ln_router",
    )(x, rg_p, ml_p, rg_s, ml_s, wo, g.reshape(1, d), b.reshape(1, d), rwt, rb.reshape(N_EXPERTS, 1))


def _dispatch_plan(e_t, tb, nb):
    n = e_t.shape[1]
    p = TOP_K * n
    i32 = jnp.int32
    flat_e = e_t.T.reshape(p)
    sk = jnp.sort(flat_e * p + jnp.arange(p, dtype=i32))
    order = sk % p
    counts = jnp.sum((flat_e[:, None] == jnp.arange(N_EXPERTS, dtype=i32)[None, :]).astype(i32), axis=0)
    start = jnp.cumsum(counts) - counts
    pcounts = (counts + tb - 1) // tb * tb
    pend = jnp.cumsum(pcounts)
    pstart = pend - pcounts
    r = jnp.arange(nb * tb, dtype=i32)
    er = jnp.minimum(jnp.sum((r[:, None] >= pend[None, :]).astype(i32), axis=1), N_EXPERTS - 1)
    off = r - pstart[er]
    real = (r < pend[-1]) & (off < counts[er])
    n_real_before = jnp.where(r < pend[-1], start[er] + jnp.minimum(off, counts[er]), p)
    pair = order[jnp.clip(start[er] + off, 0, p - 1)]
    tok, k = pair // TOP_K, pair % TOP_K
    src_tok = jnp.where(real, tok, 0).astype(i32)
    dst_row = jnp.where(real, k * n + tok, p + (r - n_real_before)).astype(i32)
    blk_e = er[::tb]
    n_used = (pend[-1] // tb).astype(i32).reshape(1)
    return src_tok, dst_row, blk_e, n_used


def _moe_kernel(be_ref, nu_ref, src_ref, srcn_ref, dst_ref, x_hbm, w1_ref, w3_ref, w2_ref, y_hbm,
                xbuf, ybuf, gsem, ssem):
    r = pl.program_id(0)
    n_r = pl.num_programs(0)
    tb = ybuf.shape[1]
    slot = r % 2
    n_used = nu_ref[0]

    def gather(idx_ref, s):
        def issue(j, carry):
            pltpu.make_async_copy(x_hbm.at[pl.ds(idx_ref[j], 1), :],
                                  xbuf.at[s, pl.ds(j, 1), :], gsem.at[s]).start()
            return carry
        lax.fori_loop(0, tb, issue, 0)

    def wait_scatter(s):
        pltpu.make_async_copy(ybuf.at[s], y_hbm.at[pl.ds(0, tb), :], ssem.at[s]).wait()

    @pl.when(r == 0)
    def _():
        gather(src_ref, 0)

    @pl.when(r + 1 < n_used)
    def _():
        gather(srcn_ref, 1 - slot)

    @pl.when(r >= 2)
    def _():
        wait_scatter(slot)

    @pl.when(r < n_used)
    def _():
        pltpu.make_async_copy(x_hbm.at[pl.ds(0, tb), :], xbuf.at[slot], gsem.at[slot]).wait()
        xb = xbuf[slot].astype(MXU_DTYPE)
        h1 = jnp.dot(xb, w1_ref[0], preferred_element_type=jnp.float32)
        h3 = jnp.dot(xb, w3_ref[0], preferred_element_type=jnp.float32)
        hb = (jax.nn.silu(h1) * h3).astype(MXU_DTYPE)
        ybuf[slot] = jnp.dot(hb, w2_ref[0], preferred_element_type=jnp.float32)

    @pl.when(r >= n_used)
    def _():
        ybuf[slot] = jnp.zeros((tb, ybuf.shape[2]), ybuf.dtype)

    def put(j, carry):
        pltpu.make_async_copy(ybuf.at[slot, pl.ds(j, 1), :],
                              y_hbm.at[pl.ds(dst_ref[j], 1), :], ssem.at[slot]).start()
        return carry
    lax.fori_loop(0, tb, put, 0)

    @pl.when(r == n_r - 1)
    def _():
        wait_scatter(1 - slot)
        wait_scatter(slot)


def _moe_call(x1, src_tok, dst_row, blk_e, n_used, w1, w3, w2, tb, nb):
    n, d = x1.shape
    f = w1.shape[-1]
    assert nb >= 2

    def wblk(r, be, nu):
        return (be[jnp.minimum(r, nu[0] - 1)], 0, 0)

    grid_spec = pltpu.PrefetchScalarGridSpec(
        num_scalar_prefetch=2,
        grid=(nb,),
        in_specs=[
            pl.BlockSpec((tb,), lambda r, be, nu: (r,), memory_space=pltpu.SMEM),
            pl.BlockSpec((tb,), lambda r, be, nu: (jnp.minimum(r + 1, nb - 1),), memory_space=pltpu.SMEM),
            pl.BlockSpec((tb,), lambda r, be, nu: (r,), memory_space=pltpu.SMEM),
            pl.BlockSpec(memory_space=pl.ANY),
            pl.BlockSpec((1, d, f), wblk),
            pl.BlockSpec((1, d, f), wblk),
            pl.BlockSpec((1, f, d), wblk),
        ],
        out_specs=pl.BlockSpec(memory_space=pl.ANY),
        scratch_shapes=[pltpu.VMEM((2, tb, d), jnp.float32), pltpu.VMEM((2, tb, d), jnp.float32),
                        pltpu.SemaphoreType.DMA((2,)), pltpu.SemaphoreType.DMA((2,))],
    )
    return pl.pallas_call(
        _moe_kernel,
        grid_spec=grid_spec,
        out_shape=jax.ShapeDtypeStruct((nb * tb, d), jnp.float32),
        compiler_params=_cparams(("arbitrary",)),
        name="moe_experts",
    )(blk_e, n_used, src_tok, src_tok, dst_row, x1, w1, w3, w2)


def _combine_math(x_ref, y0_ref, y1_ref, gc_ref, g_ref, b_ref):
    gc = gc_ref[...]
    f = y0_ref[...] * gc[:, 0:1] + y1_ref[...] * gc[:, 1:2]
    return _layer_norm(ALPHA * x_ref[...] + f, g_ref[...], b_ref[...])


def _combine_kernel(x_ref, y0_ref, y1_ref, gc_ref, g_ref, b_ref, o_ref, ob_ref):
    y = _combine_math(x_ref, y0_ref, y1_ref, gc_ref, g_ref, b_ref)
    o_ref[...] = y
    ob_ref[...] = y.astype(ob_ref.dtype)


def _combine_last_kernel(x_ref, y0_ref, y1_ref, gc_ref, g_ref, b_ref, op_ref, os_ref, *, n_prompt_tiles):
    y = _combine_math(x_ref, y0_ref, y1_ref, gc_ref, g_ref, b_ref)
    is_prompt = pl.program_id(0) < n_prompt_tiles

    @pl.when(is_prompt)
    def _():
        op_ref[...] = y

    @pl.when(jnp.logical_not(is_prompt))
    def _():
        os_ref[...] = y


def _combine_call(x1, gcol, ys, g, b, n_p=None):
    n, d = x1.shape
    tm = TM_CMB
    nt = n // tm
    row = pl.BlockSpec((tm, d), lambda i: (i, 0))
    vec = pl.BlockSpec((1, d), lambda i: (0, 0))
    in_specs = [row, row, pl.BlockSpec((tm, d), lambda i: (nt + i, 0)),
                pl.BlockSpec((tm, LANES), lambda i: (i, 0)), vec, vec]
    args = (x1, ys, ys, gcol, g.reshape(1, d), b.reshape(1, d))
    if n_p is None:
        return pl.pallas_call(
            _combine_kernel,
            grid=(nt,),
            in_specs=in_specs,
            out_specs=[row, row],
            out_shape=[jax.ShapeDtypeStruct((n, d), jnp.float32),
                       jax.ShapeDtypeStruct((n, d), MXU_DTYPE)],
            compiler_params=_cparams(("parallel",)),
            name="combine_ln",
        )(*args)
    npt = n_p // tm
    return pl.pallas_call(
        functools.partial(_combine_last_kernel, n_prompt_tiles=npt),
        grid=(nt,),
        in_specs=in_specs,
        out_specs=[pl.BlockSpec((tm, d), lambda i: (jnp.minimum(i, npt - 1), 0)),
                   pl.BlockSpec((tm, d), lambda i: (jnp.maximum(i - npt, 0), 0))],
        out_shape=[jax.ShapeDtypeStruct((n_p, d), jnp.float32),
                   jax.ShapeDtypeStruct((n - n_p, d), jnp.float32)],
        compiler_params=_cparams(("arbitrary",)),
        name="combine_ln_last",
    )(*args)


def kernel(x_prompt, x_sample, state_conv, state_lru, state_mlstm_C, state_mlstm_n, state_mlstm_m,
           ln_in_g, ln_in_b, w_in, b_in, conv_w, conv_b, lru_wa, lru_ba, lru_wx, lru_bx, lru_lambda,
           mlstm_norm_g, w_out, ln1_g, ln1_b, router_w, router_b, w1, w3, w2, ln2_g, ln2_b):
    bp, tp, d = x_prompt.shape
    bs, ts, _ = x_sample.shape
    depth = w_in.shape[0]
    n_p = bp * tp
    n_s = bs * SLOT
    n_pad = n_p + n_s
    assert d == D_MODEL and ts <= min(SLOT, CHUNK) and ts % SUBLANES == 0
    assert tp % T_RG == 0 and tp % L_ML == 0
    assert n_p % TM_PROJ == 0 and n_s % TM_PROJ == 0
    f32 = jnp.float32

    nb = -(-(TOP_K * n_pad + N_EXPERTS * (TB_MOE - 1)) // TB_MOE)

    w_in_m = w_in[:, :, :GATE_OFF].astype(MXU_DTYPE)
    w_gate_m = jnp.pad(w_in[:, :, GATE_OFF:], ((0, 0), (0, 0), (0, LANES - 2 * ML_HEADS))).astype(MXU_DTYPE)
    b_gate = jnp.pad(b_in[:, GATE_OFF:], ((0, 0), (0, LANES - 2 * ML_HEADS)))
    w_rg_m = jnp.concatenate([lru_wa, lru_wx], axis=-1).astype(MXU_DTYPE)
    w_out_m = w_out.astype(MXU_DTYPE)
    rwt_m = router_w.T.astype(MXU_DTYPE)
    w1_m, w3_m, w2_m = w1.astype(MXU_DTYPE), w3.astype(MXU_DTYPE), w2.astype(MXU_DTYPE)

    zp = lambda *s: jnp.zeros(s, f32)
    x, xb = _ln_call(x_prompt.reshape(n_p, d),
                     jnp.pad(x_sample, ((0, 0), (0, SLOT - ts), (0, 0))).reshape(n_s, d), ln_in_g, ln_in_b)
    outs = {k: [] for k in ("pc", "pl", "pC", "pn", "pm", "sc", "sl", "sC", "sn", "sm")}
    for l in range(depth):
        z3 = _in_proj_call(xb, w_in_m[l], b_in[l, :GATE_OFF].reshape(1, GATE_OFF))
        gates = _gate_proj_call(xb, w_gate_m[l], b_gate[l].reshape(1, LANES))

        rg_w = (conv_w[l], conv_b[l], w_rg_m[l], lru_ba[l], lru_bx[l], lru_lambda[l])
        rg_p, pc, plru = _rglru_call(z3, zp(bp, CONV_W - 1, D_RG), zp(bp, D_RG), *rg_w, row0=0, seq_stride=tp,
                                     tt=T_RG, tv=T_RG, n_t=tp // T_RG, name="rglru_prompt")
        rg_s, sc, slru = _rglru_call(z3, state_conv[l], state_lru[l], *rg_w, row0=n_p, seq_stride=SLOT,
                                     tt=SLOT, tv=ts, n_t=1, name="rglru_sample")
        ml_p, pC, pn, pm = _mlstm_call(z3, gates, zp(bp, ML_HEADS, ML_DH, ML_DH), zp(bp, ML_HEADS, ML_DH),
                                       zp(bp, ML_HEADS), mlstm_norm_g[l], row0=0, seq_stride=tp,
                                       ll=L_ML, lc=min(CHUNK, tp), tv=L_ML, n_c=tp // L_ML, name="mlstm_prompt")
        ml_s, sC, sn, sm = _mlstm_call(z3, gates, state_mlstm_C[l], state_mlstm_n[l], state_mlstm_m[l],
                                       mlstm_norm_g[l], row0=n_p, seq_stride=SLOT,
                                       ll=SLOT, lc=SLOT, tv=ts, n_c=1, name="mlstm_sample")

        x1, x1b, e_t, gcol = _outproj_call(x, rg_p, ml_p, rg_s, ml_s, w_out_m[l], ln1_g[l], ln1_b[l],
                                           rwt_m, router_b)
        src_tok, dst_row, blk_e, n_used = _dispatch_plan(e_t[:TOP_K], TB_MOE, nb)
        ys = _moe_call(x1, src_tok, dst_row, blk_e, n_used, w1_m[l], w3_m[l], w2_m[l], TB_MOE, nb)
        if l + 1 < depth:
            x, xb = _combine_call(x1, gcol, ys, ln2_g[l], ln2_b[l])
        else:
            y_p, y_s = _combine_call(x1, gcol, ys, ln2_g[l], ln2_b[l], n_p=n_p)

        for k, v in zip(("pc", "pl", "pC", "pn", "pm", "sc", "sl", "sC", "sn", "sm"),
                        (pc, plru, pC, pn, pm, sc, slru, sC, sn, sm)):
            outs[k].append(v)

    y_prompt = y_p.reshape(bp, tp, d)
    y_sample = y_s.reshape(bs, SLOT, d)[:, :ts]
    st = {k: jnp.stack(v) for k, v in outs.items()}
    return (y_prompt, y_sample, st["pc"], st["pl"], st["pC"], st["pn"], st["pm"],
            st["sc"], st["sl"], st["sC"], st["sn"], st["sm"])
```

```python
import functools

import jax
import jax.numpy as jnp
from jax import lax
from jax.experimental import pallas as pl
from jax.experimental.pallas import tpu as pltpu

DEPTH = 4
D_MODEL = 2048
D_RG = 1024
RG_BLOCKS = 8
RG_BW = D_RG // RG_BLOCKS
CONV_W = 4
LRU_C = 8.0
D_ML = 1024
ML_HEADS = 4
ML_DH = D_ML // ML_HEADS
CHUNK = 64
N_EXPERTS = 16
N_GROUPS = 4
EXPERTS_PER_GROUP = N_EXPERTS // N_GROUPS
TOP_K = 2
D_FF = D_MODEL // 2
ALPHA = (2 * DEPTH) ** 0.25
LN_EPS = 1e-5
RMS_EPS = 1e-6
N_PROJ = 6
GATE_OFF = N_PROJ * D_RG

LANES = 128
SUBLANES = 8
MXU_DTYPE = jnp.bfloat16
SLOT = 128
TM_PROJ = 1024
TM_LN = 512
TM_OUT = 256
TM_CMB = 512
TB_MOE = 256
T_RG = 512
L_ML = 256
VMEM_LIMIT = 56 * 1024 * 1024


def _cparams(sem):
    return pltpu.CompilerParams(dimension_semantics=sem, vmem_limit_bytes=VMEM_LIMIT)


def _layer_norm(h, g, b):
    mu = jnp.mean(h, axis=-1, keepdims=True)
    hc = h - mu
    var = jnp.mean(hc * hc, axis=-1, keepdims=True)
    return hc * lax.rsqrt(var + LN_EPS) * g + b


def _ln_kernel(xp_ref, xs_ref, g_ref, b_ref, o_ref, ob_ref, *, n_prompt_tiles):
    x = jnp.where(pl.program_id(0) < n_prompt_tiles, xp_ref[...], xs_ref[...])
    y = _layer_norm(x, g_ref[...], b_ref[...])
    o_ref[...] = y
    ob_ref[...] = y.astype(ob_ref.dtype)


def _ln_call(xp, xs, g, b):
    (n_p, d), n_s = xp.shape, xs.shape[0]
    tm = TM_LN
    npt = n_p // tm
    n = n_p + n_s
    row = pl.BlockSpec((tm, d), lambda i: (i, 0))
    vec = pl.BlockSpec((1, d), lambda i: (0, 0))
    return pl.pallas_call(
        functools.partial(_ln_kernel, n_prompt_tiles=npt),
        grid=(n // tm,),
        in_specs=[pl.BlockSpec((tm, d), lambda i: (jnp.minimum(i, npt - 1), 0)),
                  pl.BlockSpec((tm, d), lambda i: (jnp.maximum(i - npt, 0), 0)),
                  vec, vec],
        out_specs=[row, row],
        out_shape=[jax.ShapeDtypeStruct((n, d), jnp.float32),
                   jax.ShapeDtypeStruct((n, d), MXU_DTYPE)],
        compiler_params=_cparams(("parallel",)),
        name="ln_in",
    )(xp, xs, g.reshape(1, d), b.reshape(1, d))


def _proj_kernel(x_ref, w_ref, b_ref, o_ref):
    o_ref[...] = jnp.dot(x_ref[...], w_ref[...], preferred_element_type=jnp.float32) + b_ref[...]


def _in_proj_call(xb, w, b):
    n, d = xb.shape
    return pl.pallas_call(
        _proj_kernel,
        grid=(N_PROJ, n // TM_PROJ),
        in_specs=[pl.BlockSpec((TM_PROJ, d), lambda j, i: (i, 0)),
                  pl.BlockSpec((d, D_RG), lambda j, i: (0, j)),
                  pl.BlockSpec((1, D_RG), lambda j, i: (0, j))],
        out_specs=pl.BlockSpec((None, TM_PROJ, D_RG), lambda j, i: (j, i, 0)),
        out_shape=jax.ShapeDtypeStruct((N_PROJ, n, D_RG), jnp.float32),
        compiler_params=_cparams(("parallel", "parallel")),
        name="in_proj",
    )(xb, w, b)


def _gate_proj_call(xb, wg, bg):
    n, d = xb.shape
    return pl.pallas_call(
        _proj_kernel,
        grid=(n // TM_PROJ,),
        in_specs=[pl.BlockSpec((TM_PROJ, d), lambda i: (i, 0)),
                  pl.BlockSpec((d, LANES), lambda i: (0, 0)),
                  pl.BlockSpec((1, LANES), lambda i: (0, 0))],
        out_specs=pl.BlockSpec((TM_PROJ, LANES), lambda i: (i, 0)),
        out_shape=jax.ShapeDtypeStruct((n, LANES), jnp.float32),
        compiler_params=_cparams(("parallel",)),
        name="gate_proj",
    )(xb, wg, bg)


def _rglru_kernel(xr_ref, gr_ref, cs_ref, h0_ref, cw_ref, cb_ref, wg_ref, ba_ref, bx_ref, lam_ref,
                  mix_ref, cso_ref, ho_ref, ext_ref, a_ref, u_ref, hc_ref, *, tv, n_t):
    t = pl.program_id(1)
    tt = mix_ref.shape[0]

    @pl.when(t == 0)
    def _():
        ext_ref[0:SUBLANES, :] = jnp.zeros((SUBLANES, D_RG), jnp.float32)
        ext_ref[SUBLANES - (CONV_W - 1):SUBLANES, :] = cs_ref[0]
        hc_ref[...] = h0_ref[0]

    u_in = xr_ref[0:tv, :]
    ext_ref[SUBLANES:SUBLANES + tv, :] = u_in
    cw = cw_ref[...]
    xc = cb_ref[...]
    for tap in range(CONV_W - 1):
        off = SUBLANES - (CONV_W - 1) + tap
        xc = xc + ext_ref[off:off + tv, :] * cw[tap:tap + 1, :]
    xc = xc + u_in * cw[CONV_W - 1:CONV_W, :]

    @pl.when(t == n_t - 1)
    def _():
        cso_ref[0] = ext_ref[SUBLANES + tv - (CONV_W - 1):SUBLANES + tv, :]

    ext_ref[0:SUBLANES, :] = ext_ref[tv:tv + SUBLANES, :]

    xcb = xc.astype(MXU_DTYPE)
    rowmod = lax.broadcasted_iota(jnp.int32, (tv, RG_BW), 0) % SUBLANES
    for nb in range(RG_BLOCKS):
        cs = slice(nb * RG_BW, (nb + 1) * RG_BW)
        pre = jnp.dot(xcb[:, cs], wg_ref[nb], preferred_element_type=jnp.float32)
        r = jax.nn.sigmoid(pre[:, :RG_BW] + ba_ref[:, cs])
        ig = jax.nn.sigmoid(pre[:, RG_BW:] + bx_ref[:, cs])
        log_a = LRU_C * r * jax.nn.log_sigmoid(lam_ref[:, cs])
        a = jnp.exp(log_a)
        th = jnp.tanh(log_a)
        u = jnp.sqrt(-2.0 * th / (1.0 - th)) * (ig * xc[:, cs])
        for s in (1, 2, 4):
            a_s = pltpu.roll(a, s, 0)
            u_s = pltpu.roll(u, s, 0)
            m = rowmod >= s
            u = jnp.where(m, a * u_s + u, u)
            a = jnp.where(m, a * a_s, a)
        a_ref[0:tv, cs] = a
        u_ref[0:tv, cs] = u

    def body(g, carry):
        rows = pl.ds(pl.multiple_of(g * SUBLANES, SUBLANES), SUBLANES)
        h = a_ref[rows, :] * carry + u_ref[rows, :]
        u_ref[rows, :] = h
        return h[SUBLANES - 1:SUBLANES, :]

    h_last = lax.fori_loop(0, tv // SUBLANES, body, hc_ref[...])
    hc_ref[...] = h_last
    ho_ref[0] = h_last

    mix_ref[0:tv, :] = (jax.nn.gelu(gr_ref[0:tv, :]) * u_ref[0:tv, :]).astype(mix_ref.dtype)
    if tv < tt:
        mix_ref[tv:tt, :] = jnp.zeros((tt - tv, D_RG), mix_ref.dtype)


def _rglru_call(z3, conv0, h0, cw, cb, wg, ba, bx, lam, *, row0, seq_stride, tt, tv, n_t, name):
    n_seq = conv0.shape[0]
    blk0, sb = row0 // tt, seq_stride // tt

    def rows(b, t):
        return blk0 + b * sb + t

    vec = pl.BlockSpec((1, D_RG), lambda b, t: (0, 0))
    in_specs = [
        pl.BlockSpec((None, tt, D_RG), lambda b, t: (0, rows(b, t), 0)),
        pl.BlockSpec((None, tt, D_RG), lambda b, t: (1, rows(b, t), 0)),
        pl.BlockSpec((1, CONV_W - 1, D_RG), lambda b, t: (b, 0, 0)),
        pl.BlockSpec((1, 1, D_RG), lambda b, t: (b, 0, 0)),
        pl.BlockSpec((CONV_W, D_RG), lambda b, t: (0, 0)),
        vec,
        pl.BlockSpec((RG_BLOCKS, RG_BW, 2 * RG_BW), lambda b, t: (0, 0, 0)),
        vec, vec, vec,
    ]
    args = [z3, z3, conv0, h0.reshape(n_seq, 1, D_RG), cw, cb.reshape(1, D_RG), wg,
            ba.reshape(1, D_RG), bx.reshape(1, D_RG), lam.reshape(1, D_RG)]
    mix, conv_out, h_out = pl.pallas_call(
        functools.partial(_rglru_kernel, tv=tv, n_t=n_t),
        grid=(n_seq, n_t),
        in_specs=in_specs,
        out_specs=[pl.BlockSpec((tt, D_RG), lambda b, t: (b * sb + t, 0)),
                   pl.BlockSpec((1, CONV_W - 1, D_RG), lambda b, t: (b, 0, 0)),
                   pl.BlockSpec((1, 1, D_RG), lambda b, t: (b, 0, 0))],
        out_shape=[jax.ShapeDtypeStruct((n_seq * seq_stride, D_RG), MXU_DTYPE),
                   jax.ShapeDtypeStruct((n_seq, CONV_W - 1, D_RG), jnp.float32),
                   jax.ShapeDtypeStruct((n_seq, 1, D_RG), jnp.float32)],
        scratch_shapes=[pltpu.VMEM((tt + SUBLANES, D_RG), jnp.float32),
                        pltpu.VMEM((tt, D_RG), jnp.float32),
                        pltpu.VMEM((tt, D_RG), jnp.float32),
                        pltpu.VMEM((1, D_RG), jnp.float32)],
        compiler_params=_cparams(("parallel", "arbitrary")),
        name=name,
    )(*args)
    return mix, conv_out, h_out.reshape(n_seq, D_RG)


def _mlstm_kernel(q_ref, k_ref, v_ref, o_ref, g_ref, c0_ref, n0_ref, m0_ref, ng_ref,
                  mix_ref, co_ref, no_ref, mo_ref, c_s, n_s, m_s, *, lc_last, tv, n_c):
    c = pl.program_id(1)
    ll = mix_ref.shape[0]
    h_n = ML_HEADS

    @pl.when(c == 0)
    def _():
        c_s[...] = c0_ref[0]
        n_s[...] = n0_ref[0]
        m_s[...] = m0_ref[0]

    dn_t = (((1,), (1,)), ((), ()))
    dn_c0 = (((0,), (0,)), ((), ()))

    def run(lc):
        g = g_ref[...]
        row = lax.broadcasted_iota(jnp.int32, (ll, LANES), 0)
        lf = jax.nn.log_sigmoid(g)
        if tv < ll:
            lane = lax.broadcasted_iota(jnp.int32, (ll, LANES), 1)
            g = jnp.where((row >= tv) & (lane < h_n), -jnp.inf, g)
            lf = jnp.where(row >= tv, 0.0, lf)
        rowc = row % lc
        bcum = lf
        s = 1
        while s < lc:
            bcum = bcum + jnp.where(rowc >= s, pltpu.roll(bcum, s, 0), 0.0)
            s *= 2
        g_t = g.T
        b_t = bcum.T
        ti = lax.broadcasted_iota(jnp.int32, (lc, lc), 0)
        si = lax.broadcasted_iota(jnp.int32, (lc, lc), 1)
        causal = si <= ti

        m_val = [m_s[:, h:h + 1] for h in range(h_n)]
        n_val = [n_s[h:h + 1, :] for h in range(h_n)]
        for j in range(ll // lc):
            rs = slice(j * lc, (j + 1) * lc)
            for h in range(h_n):
                cs = slice(h * ML_DH, (h + 1) * ML_DH)
                qf = q_ref[rs, cs]
                kf = k_ref[rs, cs] * (ML_DH ** -0.5)
                vf = v_ref[rs, cs]
                qb, kb, vb = qf.astype(MXU_DTYPE), kf.astype(MXU_DTYPE), vf.astype(MXU_DTYPE)
                ig_col = g[rs, h:h + 1]
                b_col = bcum[rs, h_n + h:h_n + h + 1]
                ig_row = g_t[h:h + 1, rs]
                b_row = b_t[h_n + h:h_n + h + 1, rs]
                m_prev = m_val[h]

                logw = jnp.where(causal, b_col - b_row + ig_row, -jnp.inf)
                gq = b_col + m_prev
                mt = jnp.maximum(gq, jnp.max(logw, axis=-1, keepdims=True))
                w_intra = jnp.exp(logw - mt)
                w_state = jnp.exp(gq - mt)
                s_mat = lax.dot_general(qb, kb, dn_t, preferred_element_type=jnp.float32) * w_intra
                c_prev = c_s[h]
                n_prev = n_val[h]
                num = (w_state * jnp.dot(qb, c_prev.astype(MXU_DTYPE), preferred_element_type=jnp.float32)
                       + jnp.dot(s_mat.astype(MXU_DTYPE), vb, preferred_element_type=jnp.float32))
                den = (w_state * jnp.sum(qf * n_prev, axis=-1, keepdims=True)
                       + jnp.sum(s_mat, axis=-1, keepdims=True))
                hh = num / jnp.maximum(jnp.abs(den), jnp.exp(-mt))

                b_last = b_col[lc - 1:lc, :]
                logu_row = b_last - b_row + ig_row
                m_new = jnp.maximum(b_last + m_prev, jnp.max(logu_row, axis=-1, keepdims=True))
                u_col = jnp.exp(b_last - b_col + ig_col - m_new)
                decay = jnp.exp(b_last + m_prev - m_new)
                ku = kf * u_col
                c_s[h] = decay * c_prev + lax.dot_general(ku.astype(MXU_DTYPE), vb, dn_c0,
                                                          preferred_element_type=jnp.float32)
                n_val[h] = decay * n_prev + jnp.sum(ku, axis=0, keepdims=True)
                m_val[h] = m_new

                hm = hh * lax.rsqrt(jnp.mean(hh * hh, axis=-1, keepdims=True) + RMS_EPS) * ng_ref[:, cs]
                mix_ref[rs, cs] = (hm * jax.nn.sigmoid(o_ref[rs, cs])).astype(mix_ref.dtype)

        for h in range(h_n):
            m_s[:, h:h + 1] = m_val[h]
            n_s[h:h + 1, :] = n_val[h]

    if lc_last == ll:
        run(ll)
    else:
        @pl.when(c < n_c - 1)
        def _():
            run(ll)

        @pl.when(c == n_c - 1)
        def _():
            run(lc_last)

    @pl.when(c == n_c - 1)
    def _():
        co_ref[0] = c_s[...]
        no_ref[0] = n_s[...]
        mo_ref[0] = m_s[...]


def _mlstm_call(z3, gates, c0, n0, m0, ng, *, row0, seq_stride, ll, lc_last, tv, n_c, name):
    n_seq = c0.shape[0]
    blk0, sb = row0 // ll, seq_stride // ll

    def rows(b, c):
        return blk0 + b * sb + c

    def zspec(j):
        return pl.BlockSpec((None, ll, D_ML), lambda b, c: (j, rows(b, c), 0))

    in_specs = [
        zspec(2), zspec(3), zspec(4), zspec(5),
        pl.BlockSpec((ll, LANES), lambda b, c: (rows(b, c), 0)),
        pl.BlockSpec((1, ML_HEADS, ML_DH, ML_DH), lambda b, c: (b, 0, 0, 0)),
        pl.BlockSpec((1, ML_HEADS, ML_DH), lambda b, c: (b, 0, 0)),
        pl.BlockSpec((1, 1, ML_HEADS), lambda b, c: (b, 0, 0)),
        pl.BlockSpec((1, D_ML), lambda b, c: (0, 0)),
    ]
    args = [z3, z3, z3, z3, gates, c0, n0, m0.reshape(n_seq, 1, ML_HEADS), ng.reshape(1, D_ML)]
    mix, c_out, n_out, m_out = pl.pallas_call(
        functools.partial(_mlstm_kernel, lc_last=lc_last, tv=tv, n_c=n_c),
        grid=(n_seq, n_c),
        in_specs=in_specs,
        out_specs=[pl.BlockSpec((ll, D_ML), lambda b, c: (b * sb + c, 0)),
                   pl.BlockSpec((1, ML_HEADS, ML_DH, ML_DH), lambda b, c: (b, 0, 0, 0)),
                   pl.BlockSpec((1, ML_HEADS, ML_DH), lambda b, c: (b, 0, 0)),
                   pl.BlockSpec((1, 1, ML_HEADS), lambda b, c: (b, 0, 0))],
        out_shape=[jax.ShapeDtypeStruct((n_seq * seq_stride, D_ML), MXU_DTYPE),
                   jax.ShapeDtypeStruct((n_seq, ML_HEADS, ML_DH, ML_DH), jnp.float32),
                   jax.ShapeDtypeStruct((n_seq, ML_HEADS, ML_DH), jnp.float32),
                   jax.ShapeDtypeStruct((n_seq, 1, ML_HEADS), jnp.float32)],
        scratch_shapes=[pltpu.VMEM((ML_HEADS, ML_DH, ML_DH), jnp.float32),
                        pltpu.VMEM((ML_HEADS, ML_DH), jnp.float32),
                        pltpu.VMEM((1, ML_HEADS), jnp.float32)],
        compiler_params=_cparams(("parallel", "arbitrary")),
        name=name,
    )(*args)
    return mix, c_out, n_out, m_out.reshape(n_seq, ML_HEADS)


def _first_argmax(vals):
    best, idx = vals[0], jnp.zeros(vals[0].shape, jnp.int32)
    for j in range(1, len(vals)):
        better = vals[j] > best
        best = jnp.where(better, vals[j], best)
        idx = jnp.where(better, j, idx)
    return best, idx


def _pick(vals, idx):
    out = vals[0]
    for j in range(1, len(vals)):
        out = jnp.where(idx == j, vals[j], out)
    return out


def _outproj_kernel(x_ref, rgp_ref, mlp_ref, rgs_ref, mls_ref, wo_ref, g_ref, b_ref, rwt_ref, rb_ref,
                    x1_ref, x1b_ref, e_ref, gc_ref, *, n_prompt_tiles):
    tm = x_ref.shape[0]
    is_prompt = pl.program_id(0) < n_prompt_tiles
    mix_rg = jnp.where(is_prompt, rgp_ref[...], rgs_ref[...])
    mix_ml = jnp.where(is_prompt, mlp_ref[...], mls_ref[...])
    y = (jnp.dot(mix_rg, wo_ref[0:D_RG, :], preferred_element_type=jnp.float32)
         + jnp.dot(mix_ml, wo_ref[D_RG:, :], preferred_element_type=jnp.float32))
    x1 = _layer_norm(ALPHA * x_ref[...] + y, g_ref[...], b_ref[...])
    x1_ref[...] = x1
    x1b = x1.astype(x1b_ref.dtype)
    x1b_ref[...] = x1b

    logits = lax.dot_general(rwt_ref[...], x1b, (((1,), (1,)), ((), ())),
                             preferred_element_type=jnp.float32)
    ex = jnp.exp(logits - jnp.max(logits, axis=0, keepdims=True))
    probs = ex / jnp.sum(ex, axis=0, keepdims=True)
    sel = probs + rb_ref[...]
    sel_r = [sel[e:e + 1, :] for e in range(N_EXPERTS)]
    prob_r = [probs[e:e + 1, :] for e in range(N_EXPERTS)]
    scores = []
    for gi in range(N_GROUPS):
        v = sel_r[gi * EXPERTS_PER_GROUP:(gi + 1) * EXPERTS_PER_GROUP]
        top2 = None
        for i in range(EXPERTS_PER_GROUP):
            for j in range(i + 1, EXPERTS_PER_GROUP):
                pair = v[i] + v[j]
                top2 = pair if top2 is None else jnp.maximum(top2, pair)
        scores.append(top2)
    _, g_idx = _first_argmax(scores)
    in_sel = [_pick([sel_r[gi * EXPERTS_PER_GROUP + j] for gi in range(N_GROUPS)], g_idx)
              for j in range(EXPERTS_PER_GROUP)]
    in_prob = [_pick([prob_r[gi * EXPERTS_PER_GROUP + j] for gi in range(N_GROUPS)], g_idx)
               for j in range(EXPERTS_PER_GROUP)]
    _, loc0 = _first_argmax(in_sel)
    masked = [jnp.where(loc0 == j, -jnp.inf, in_sel[j]) for j in range(EXPERTS_PER_GROUP)]
    _, loc1 = _first_argmax(masked)
    p0, p1 = _pick(in_prob, loc0), _pick(in_prob, loc1)
    psum = p0 + p1
    e0 = g_idx * EXPERTS_PER_GROUP + loc0
    e1 = g_idx * EXPERTS_PER_GROUP + loc1
    e_ref[...] = jnp.concatenate([e0, e1, jnp.zeros((SUBLANES - TOP_K, tm), jnp.int32)], axis=0)
    gates_t = jnp.concatenate([p0 / psum, p1 / psum, jnp.zeros((LANES - TOP_K, tm), jnp.float32)], axis=0)
    gc_ref[...] = gates_t.T


def _outproj_call(x, rg_p, ml_p, rg_s, ml_s, wo, g, b, rwt, rb):
    n, d = x.shape
    tm = TM_OUT
    npt = rg_p.shape[0] // tm
    row = pl.BlockSpec((tm, d), lambda i: (i, 0))
    half_p = pl.BlockSpec((tm, D_RG), lambda i: (jnp.minimum(i, npt - 1), 0))
    half_s = pl.BlockSpec((tm, D_RG), lambda i: (jnp.maximum(i - npt, 0), 0))
    vec = pl.BlockSpec((1, d), lambda i: (0, 0))
    return pl.pallas_call(
        functools.partial(_outproj_kernel, n_prompt_tiles=npt),
        grid=(n // tm,),
        in_specs=[row, half_p, half_p, half_s, half_s,
                  pl.BlockSpec((d, d), lambda i: (0, 0)),
                  vec, vec,
                  pl.BlockSpec((N_EXPERTS, d), lambda i: (0, 0)),
                  pl.BlockSpec((N_EXPERTS, 1), lambda i: (0, 0))],
        out_specs=[row, row,
                   pl.BlockSpec((SUBLANES, tm), lambda i: (0, i)),
                   pl.BlockSpec((tm, LANES), lambda i: (i, 0))],
        out_shape=[jax.ShapeDtypeStruct((n, d), jnp.float32),
                   jax.ShapeDtypeStruct((n, d), MXU_DTYPE),
                   jax.ShapeDtypeStruct((SUBLANES, n), jnp.int32),
                   jax.ShapeDtypeStruct((n, LANES), jnp.float32)],
        compiler_params=_cparams(("parallel",)),
        name="outproj_ln_router",
    )(x, rg_p, ml_p, rg_s, ml_s, wo, g.reshape(1, d), b.reshape(1, d), rwt, rb.reshape(N_EXPERTS, 1))


def _dispatch_plan(e_t, tb, nb):
    n = e_t.shape[1]
    p = TOP_K * n
    i32 = jnp.int32
    flat_e = e_t.T.reshape(p)
    sk = jnp.sort(flat_e * p + jnp.arange(p, dtype=i32))
    order = sk % p
    counts = jnp.sum((flat_e[:, None] == jnp.arange(N_EXPERTS, dtype=i32)[None, :]).astype(i32), axis=0)
    start = jnp.cumsum(counts) - counts
    pcounts = (counts + tb - 1) // tb * tb
    pend = jnp.cumsum(pcounts)
    pstart = pend - pcounts
    r = jnp.arange(nb * tb, dtype=i32)
    er = jnp.minimum(jnp.sum((r[:, None] >= pend[None, :]).astype(i32), axis=1), N_EXPERTS - 1)
    off = r - pstart[er]
    real = (r < pend[-1]) & (off < counts[er])
    n_real_before = jnp.where(r < pend[-1], start[er] + jnp.minimum(off, counts[er]), p)
    pair = order[jnp.clip(start[er] + off, 0, p - 1)]
    tok, k = pair // TOP_K, pair % TOP_K
    src_tok = jnp.where(real, tok, 0).astype(i32)
    dst_row = jnp.where(real, k * n + tok, p + (r - n_real_before)).astype(i32)
    blk_e = er[::tb]
    return src_tok, dst_row, blk_e


def _moe_kernel(be_ref, src_ref, srcn_ref, dst_ref, dstp_ref, x_hbm, w1_ref, w3_ref, w2_ref, y_hbm,
                xbuf, ybuf, gsem, ssem):
    r = pl.program_id(0)
    n_r = pl.num_programs(0)
    tb = ybuf.shape[1]
    slot = r % 2
    other = 1 - slot

    def gather_row(idx_ref, s, j):
        pltpu.make_async_copy(x_hbm.at[pl.ds(idx_ref[j], 1), :],
                              xbuf.at[s, pl.ds(j, 1), :], gsem.at[s]).start()

    def scatter_row(idx_ref, s, j):
        pltpu.make_async_copy(ybuf.at[s, pl.ds(j, 1), :],
                              y_hbm.at[pl.ds(idx_ref[j], 1), :], ssem.at[s]).start()

    def wait_gather(s):
        pltpu.make_async_copy(x_hbm.at[pl.ds(0, tb), :], xbuf.at[s], gsem.at[s]).wait()

    def wait_scatter(s):
        pltpu.make_async_copy(ybuf.at[s], y_hbm.at[pl.ds(0, tb), :], ssem.at[s]).wait()

    def expert(s):
        xb = xbuf[s].astype(MXU_DTYPE)
        h1 = jnp.dot(xb, w1_ref[0], preferred_element_type=jnp.float32)
        h3 = jnp.dot(xb, w3_ref[0], preferred_element_type=jnp.float32)
        hb = (jax.nn.silu(h1) * h3).astype(MXU_DTYPE)
        ybuf[s] = jnp.dot(hb, w2_ref[0], preferred_element_type=jnp.float32)

    @pl.when(r == 0)
    def _():
        for j in range(tb):
            gather_row(src_ref, 0, j)
        for j in range(tb):
            gather_row(srcn_ref, 1, j)
        wait_gather(0)
        expert(0)

    @pl.when(r > 0)
    def _():
        for j in range(tb):
            gather_row(srcn_ref, other, j)
            scatter_row(dstp_ref, other, j)
        wait_gather(slot)
        expert(slot)
        wait_scatter(other)

    @pl.when(r == n_r - 1)
    def _():
        wait_gather(other)

        def put(j, carry):
            scatter_row(dst_ref, slot, j)
            return carry
        lax.fori_loop(0, tb, put, 0)
        wait_scatter(slot)


def _moe_call(x1, src_tok, dst_row, blk_e, w1, w3, w2, tb, nb):
    n, d = x1.shape
    f = w1.shape[-1]
    assert nb >= 2

    def wblk(r, be):
        return (be[r], 0, 0)

    grid_spec = pltpu.PrefetchScalarGridSpec(
        num_scalar_prefetch=1,
        grid=(nb,),
        in_specs=[
            pl.BlockSpec((tb,), lambda r, be: (r,), memory_space=pltpu.SMEM),
            pl.BlockSpec((tb,), lambda r, be: (jnp.minimum(r + 1, nb - 1),), memory_space=pltpu.SMEM),
            pl.BlockSpec((tb,), lambda r, be: (r,), memory_space=pltpu.SMEM),
            pl.BlockSpec((tb,), lambda r, be: (jnp.maximum(r - 1, 0),), memory_space=pltpu.SMEM),
            pl.BlockSpec(memory_space=pl.ANY),
            pl.BlockSpec((1, d, f), wblk),
            pl.BlockSpec((1, d, f), wblk),
            pl.BlockSpec((1, f, d), wblk),
        ],
        out_specs=pl.BlockSpec(memory_space=pl.ANY),
        scratch_shapes=[pltpu.VMEM((2, tb, d), jnp.float32), pltpu.VMEM((2, tb, d), jnp.float32),
                        pltpu.SemaphoreType.DMA((2,)), pltpu.SemaphoreType.DMA((2,))],
    )
    return pl.pallas_call(
        _moe_kernel,
        grid_spec=grid_spec,
        out_shape=jax.ShapeDtypeStruct((nb * tb, d), jnp.float32),
        compiler_params=_cparams(("arbitrary",)),
        name="moe_experts",
    )(blk_e, src_tok, src_tok, dst_row, dst_row, x1, w1, w3, w2)


def _combine_math(x_ref, y0_ref, y1_ref, gc_ref, g_ref, b_ref):
    gc = gc_ref[...]
    f = y0_ref[...] * gc[:, 0:1] + y1_ref[...] * gc[:, 1:2]
    return _layer_norm(ALPHA * x_ref[...] + f, g_ref[...], b_ref[...])


def _combine_kernel(x_ref, y0_ref, y1_ref, gc_ref, g_ref, b_ref, o_ref, ob_ref):
    y = _combine_math(x_ref, y0_ref, y1_ref, gc_ref, g_ref, b_ref)
    o_ref[...] = y
    ob_ref[...] = y.astype(ob_ref.dtype)


def _combine_last_kernel(x_ref, y0_ref, y1_ref, gc_ref, g_ref, b_ref, op_ref, os_ref, *, n_prompt_tiles):
    y = _combine_math(x_ref, y0_ref, y1_ref, gc_ref, g_ref, b_ref)
    is_prompt = pl.program_id(0) < n_prompt_tiles

    @pl.when(is_prompt)
    def _():
        op_ref[...] = y

    @pl.when(jnp.logical_not(is_prompt))
    def _():
        os_ref[...] = y


def _combine_call(x1, gcol, ys, g, b, n_p=None):
    n, d = x1.shape
    tm = TM_CMB
    nt = n // tm
    row = pl.BlockSpec((tm, d), lambda i: (i, 0))
    vec = pl.BlockSpec((1, d), lambda i: (0, 0))
    in_specs = [row, row, pl.BlockSpec((tm, d), lambda i: (nt + i, 0)),
                pl.BlockSpec((tm, LANES), lambda i: (i, 0)), vec, vec]
    args = (x1, ys, ys, gcol, g.reshape(1, d), b.reshape(1, d))
    if n_p is None:
        return pl.pallas_call(
            _combine_kernel,
            grid=(nt,),
            in_specs=in_specs,
            out_specs=[row, row],
            out_shape=[jax.ShapeDtypeStruct((n, d), jnp.float32),
                       jax.ShapeDtypeStruct((n, d), MXU_DTYPE)],
            compiler_params=_cparams(("parallel",)),
            name="combine_ln",
        )(*args)
    npt = n_p // tm
    return pl.pallas_call(
        functools.partial(_combine_last_kernel, n_prompt_tiles=npt),
        grid=(nt,),
        in_specs=in_specs,
        out_specs=[pl.BlockSpec((tm, d), lambda i: (jnp.minimum(i, npt - 1), 0)),
                   pl.BlockSpec((tm, d), lambda i: (jnp.maximum(i - npt, 0), 0))],
        out_shape=[jax.ShapeDtypeStruct((n_p, d), jnp.float32),
                   jax.ShapeDtypeStruct((n - n_p, d), jnp.float32)],
        compiler_params=_cparams(("arbitrary",)),
        name="combine_ln_last",
    )(*args)


def kernel(x_prompt, x_sample, state_conv, state_lru, state_mlstm_C, state_mlstm_n, state_mlstm_m,
           ln_in_g, ln_in_b, w_in, b_in, conv_w, conv_b, lru_wa, lru_ba, lru_wx, lru_bx, lru_lambda,
           mlstm_norm_g, w_out, ln1_g, ln1_b, router_w, router_b, w1, w3, w2, ln2_g, ln2_b):
    bp, tp, d = x_prompt.shape
    bs, ts, _ = x_sample.shape
    depth = w_in.shape[0]
    n_p = bp * tp
    n_s = bs * SLOT
    n_pad = n_p + n_s
    assert d == D_MODEL and ts <= min(SLOT, CHUNK) and ts % SUBLANES == 0
    assert tp % T_RG == 0 and tp % L_ML == 0
    assert n_p % TM_PROJ == 0 and n_s % TM_PROJ == 0
    f32 = jnp.float32

    nb = -(-(TOP_K * n_pad + N_EXPERTS * (TB_MOE - 1)) // TB_MOE)

    w_in_m = w_in[:, :, :GATE_OFF].astype(MXU_DTYPE)
    w_gate_m = jnp.pad(w_in[:, :, GATE_OFF:], ((0, 0), (0, 0), (0, LANES - 2 * ML_HEADS))).astype(MXU_DTYPE)
    b_gate = jnp.pad(b_in[:, GATE_OFF:], ((0, 0), (0, LANES - 2 * ML_HEADS)))
    w_rg_m = jnp.concatenate([lru_wa, lru_wx], axis=-1).astype(MXU_DTYPE)
    w_out_m = w_out.astype(MXU_DTYPE)
    rwt_m = router_w.T.astype(MXU_DTYPE)
    w1_m, w3_m, w2_m = w1.astype(MXU_DTYPE), w3.astype(MXU_DTYPE), w2.astype(MXU_DTYPE)

    zp = lambda *s: jnp.zeros(s, f32)
    x, xb = _ln_call(x_prompt.reshape(n_p, d),
                     jnp.pad(x_sample, ((0, 0), (0, SLOT - ts), (0, 0))).reshape(n_s, d), ln_in_g, ln_in_b)
    outs = {k: [] for k in ("pc", "pl", "pC", "pn", "pm", "sc", "sl", "sC", "sn", "sm")}
    for l in range(depth):
        z3 = _in_proj_call(xb, w_in_m[l], b_in[l, :GATE_OFF].reshape(1, GATE_OFF))
        gates = _gate_proj_call(xb, w_gate_m[l], b_gate[l].reshape(1, LANES))

        rg_w = (conv_w[l], conv_b[l], w_rg_m[l], lru_ba[l], lru_bx[l], lru_lambda[l])
        rg_p, pc, plru = _rglru_call(z3, zp(bp, CONV_W - 1, D_RG), zp(bp, D_RG), *rg_w, row0=0, seq_stride=tp,
                                     tt=T_RG, tv=T_RG, n_t=tp // T_RG, name="rglru_prompt")
        rg_s, sc, slru = _rglru_call(z3, state_conv[l], state_lru[l], *rg_w, row0=n_p, seq_stride=SLOT,
                                     tt=SLOT, tv=ts, n_t=1, name="rglru_sample")
        ml_p, pC, pn, pm = _mlstm_call(z3, gates, zp(bp, ML_HEADS, ML_DH, ML_DH), zp(bp, ML_HEADS, ML_DH),
                                       zp(bp, ML_HEADS), mlstm_norm_g[l], row0=0, seq_stride=tp,
                                       ll=L_ML, lc_last=min(CHUNK, tp), tv=L_ML, n_c=tp // L_ML, name="mlstm_prompt")
        ml_s, sC, sn, sm = _mlstm_call(z3, gates, state_mlstm_C[l], state_mlstm_n[l], state_mlstm_m[l],
                                       mlstm_norm_g[l], row0=n_p, seq_stride=SLOT,
                                       ll=SLOT, lc_last=SLOT, tv=ts, n_c=1, name="mlstm_sample")

        x1, x1b, e_t, gcol = _outproj_call(x, rg_p, ml_p, rg_s, ml_s, w_out_m[l], ln1_g[l], ln1_b[l],
                                           rwt_m, router_b)
        src_tok, dst_row, blk_e = _dispatch_plan(e_t[:TOP_K], TB_MOE, nb)
        ys = _moe_call(x1, src_tok, dst_row, blk_e, w1_m[l], w3_m[l], w2_m[l], TB_MOE, nb)
        if l + 1 < depth:
            x, xb = _combine_call(x1, gcol, ys, ln2_g[l], ln2_b[l])
        else:
            y_p, y_s = _combine_call(x1, gcol, ys, ln2_g[l], ln2_b[l], n_p=n_p)

        for k, v in zip(("pc", "pl", "pC", "pn", "pm", "sc", "sl", "sC", "sn", "sm"),
                        (pc, plru, pC, pn, pm, sc, slru, sC, sn, sm)):
            outs[k].append(v)

    y_prompt = y_p.reshape(bp, tp, d)
    y_sample = y_s.reshape(bs, SLOT, d)[:, :ts]
    st = {k: jnp.stack(v) for k, v in outs.items()}
    return (y_prompt, y_sample, st["pc"], st["pl"], st["pC"], st["pn"], st["pm"],
            st["sc"], st["sl"], st["sC"], st["sn"], st["sm"])
```

```python
import functools

import jax
import jax.numpy as jnp
from jax import lax
from jax.experimental import pallas as pl
from jax.experimental.pallas import tpu as pltpu

DEPTH = 4
D_MODEL = 2048
D_RG = 1024
RG_BLOCKS = 8
RG_BW = D_RG // RG_BLOCKS
CONV_W = 4
LRU_C = 8.0
D_ML = 1024
ML_HEADS = 4
ML_DH = D_ML // ML_HEADS
CHUNK = 64
N_EXPERTS = 16
N_GROUPS = 4
EXPERTS_PER_GROUP = N_EXPERTS // N_GROUPS
TOP_K = 2
D_FF = D_MODEL // 2
ALPHA = (2 * DEPTH) ** 0.25
LN_EPS = 1e-5
RMS_EPS = 1e-6
N_PROJ = 6
GATE_OFF = N_PROJ * D_RG

LANES = 128
SUBLANES = 8
MXU_DTYPE = jnp.bfloat16
SLOT = 128
TM_PROJ = 1024
TM_LN = 512
TM_OUT = 512
TM_CMB = 512
TB_MOE = 256
T_RG = 512
L_ML = 256
VMEM_LIMIT = 56 * 1024 * 1024


def _cparams(sem):
    return pltpu.CompilerParams(dimension_semantics=sem, vmem_limit_bytes=VMEM_LIMIT)


def _layer_norm(h, g, b):
    mu = jnp.mean(h, axis=-1, keepdims=True)
    hc = h - mu
    var = jnp.mean(hc * hc, axis=-1, keepdims=True)
    return hc * lax.rsqrt(var + LN_EPS) * g + b


def _ln_kernel(xp_ref, xs_ref, g_ref, b_ref, o_ref, ob_ref, *, n_prompt_tiles):
    x = jnp.where(pl.program_id(0) < n_prompt_tiles, xp_ref[...], xs_ref[...])
    y = _layer_norm(x, g_ref[...], b_ref[...])
    o_ref[...] = y
    ob_ref[...] = y.astype(ob_ref.dtype)


def _ln_call(xp, xs, g, b):
    (n_p, d), n_s = xp.shape, xs.shape[0]
    tm = TM_LN
    npt = n_p // tm
    n = n_p + n_s
    row = pl.BlockSpec((tm, d), lambda i: (i, 0))
    vec = pl.BlockSpec((1, d), lambda i: (0, 0))
    return pl.pallas_call(
        functools.partial(_ln_kernel, n_prompt_tiles=npt),
        grid=(n // tm,),
        in_specs=[pl.BlockSpec((tm, d), lambda i: (jnp.minimum(i, npt - 1), 0)),
                  pl.BlockSpec((tm, d), lambda i: (jnp.maximum(i - npt, 0), 0)),
                  vec, vec],
        out_specs=[row, row],
        out_shape=[jax.ShapeDtypeStruct((n, d), jnp.float32),
                   jax.ShapeDtypeStruct((n, d), MXU_DTYPE)],
        compiler_params=_cparams(("parallel",)),
        name="ln_in",
    )(xp, xs, g.reshape(1, d), b.reshape(1, d))


def _proj_kernel(x_ref, w_ref, b_ref, o_ref):
    o_ref[...] = jnp.dot(x_ref[...], w_ref[...], preferred_element_type=jnp.float32) + b_ref[...]


def _in_proj_kernel(x_ref, w_ref, b_ref, o_ref, wb_ref):
    @pl.when(pl.program_id(1) == 0)
    def _():
        wb_ref[...] = w_ref[...].astype(wb_ref.dtype)

    o_ref[...] = jnp.dot(x_ref[...], wb_ref[...], preferred_element_type=jnp.float32) + b_ref[...]


def _in_proj_call(xb, w_in, b_in, layer):
    n, d = xb.shape
    depth, _, d_in = w_in.shape
    return pl.pallas_call(
        _in_proj_kernel,
        grid=(N_PROJ, n // TM_PROJ),
        in_specs=[pl.BlockSpec((TM_PROJ, d), lambda j, i: (i, 0)),
                  pl.BlockSpec((None, d, D_RG), lambda j, i: (layer, 0, j)),
                  pl.BlockSpec((None, 1, D_RG), lambda j, i: (layer, 0, j))],
        out_specs=pl.BlockSpec((None, TM_PROJ, D_RG), lambda j, i: (j, i, 0)),
        out_shape=jax.ShapeDtypeStruct((N_PROJ, n, D_RG), jnp.float32),
        scratch_shapes=[pltpu.VMEM((d, D_RG), MXU_DTYPE)],
        compiler_params=_cparams(("parallel", "arbitrary")),
        name="in_proj",
    )(xb, w_in, b_in.reshape(depth, 1, d_in))


def _gate_proj_call(xb, wg, bg):
    n, d = xb.shape
    return pl.pallas_call(
        _proj_kernel,
        grid=(n // TM_PROJ,),
        in_specs=[pl.BlockSpec((TM_PROJ, d), lambda i: (i, 0)),
                  pl.BlockSpec((d, LANES), lambda i: (0, 0)),
                  pl.BlockSpec((1, LANES), lambda i: (0, 0))],
        out_specs=pl.BlockSpec((TM_PROJ, LANES), lambda i: (i, 0)),
        out_shape=jax.ShapeDtypeStruct((n, LANES), jnp.float32),
        compiler_params=_cparams(("parallel",)),
        name="gate_proj",
    )(xb, wg, bg)


def _rglru_kernel(xr_ref, gr_ref, cs_ref, h0_ref, cw_ref, cb_ref, wg_ref, ba_ref, bx_ref, lam_ref,
                  mix_ref, cso_ref, ho_ref, ext_ref, a_ref, u_ref, hc_ref, *, tv, n_t):
    t = pl.program_id(1)
    tt = mix_ref.shape[0]

    @pl.when(t == 0)
    def _():
        ext_ref[0:SUBLANES, :] = jnp.zeros((SUBLANES, D_RG), jnp.float32)
        ext_ref[SUBLANES - (CONV_W - 1):SUBLANES, :] = cs_ref[0]
        hc_ref[...] = h0_ref[0]

    u_in = xr_ref[0:tv, :]
    ext_ref[SUBLANES:SUBLANES + tv, :] = u_in
    cw = cw_ref[...]
    xc = cb_ref[...]
    for tap in range(CONV_W - 1):
        off = SUBLANES - (CONV_W - 1) + tap
        xc = xc + ext_ref[off:off + tv, :] * cw[tap:tap + 1, :]
    xc = xc + u_in * cw[CONV_W - 1:CONV_W, :]

    @pl.when(t == n_t - 1)
    def _():
        cso_ref[0] = ext_ref[SUBLANES + tv - (CONV_W - 1):SUBLANES + tv, :]

    ext_ref[0:SUBLANES, :] = ext_ref[tv:tv + SUBLANES, :]

    xcb = xc.astype(MXU_DTYPE)
    rowmod = lax.broadcasted_iota(jnp.int32, (tv, RG_BW), 0) % SUBLANES
    for nb in range(RG_BLOCKS):
        cs = slice(nb * RG_BW, (nb + 1) * RG_BW)
        pre = jnp.dot(xcb[:, cs], wg_ref[nb], preferred_element_type=jnp.float32)
        r = jax.nn.sigmoid(pre[:, :RG_BW] + ba_ref[:, cs])
        ig = jax.nn.sigmoid(pre[:, RG_BW:] + bx_ref[:, cs])
        log_a = LRU_C * r * jax.nn.log_sigmoid(lam_ref[:, cs])
        a = jnp.exp(log_a)
        th = jnp.tanh(log_a)
        u = jnp.sqrt(-2.0 * th / (1.0 - th)) * (ig * xc[:, cs])
        for s in (1, 2, 4):
            a_s = pltpu.roll(a, s, 0)
            u_s = pltpu.roll(u, s, 0)
            m = rowmod >= s
            u = jnp.where(m, a * u_s + u, u)
            a = jnp.where(m, a * a_s, a)
        a_ref[0:tv, cs] = a
        u_ref[0:tv, cs] = u

    def body(g, carry):
        rows = pl.ds(pl.multiple_of(g * SUBLANES, SUBLANES), SUBLANES)
        h = a_ref[rows, :] * carry + u_ref[rows, :]
        u_ref[rows, :] = h
        return h[SUBLANES - 1:SUBLANES, :]

    h_last = lax.fori_loop(0, tv // SUBLANES, body, hc_ref[...])
    hc_ref[...] = h_last
    ho_ref[0] = h_last

    mix_ref[0:tv, :] = (jax.nn.gelu(gr_ref[0:tv, :]) * u_ref[0:tv, :]).astype(mix_ref.dtype)
    if tv < tt:
        mix_ref[tv:tt, :] = jnp.zeros((tt - tv, D_RG), mix_ref.dtype)


def _rglru_call(z3, conv0, h0, cw, cb, wg, ba, bx, lam, *, row0, seq_stride, tt, tv, n_t, name):
    n_seq = conv0.shape[0]
    blk0, sb = row0 // tt, seq_stride // tt

    def rows(b, t):
        return blk0 + b * sb + t

    vec = pl.BlockSpec((1, D_RG), lambda b, t: (0, 0))
    in_specs = [
        pl.BlockSpec((None, tt, D_RG), lambda b, t: (0, rows(b, t), 0)),
        pl.BlockSpec((None, tt, D_RG), lambda b, t: (1, rows(b, t), 0)),
        pl.BlockSpec((1, CONV_W - 1, D_RG), lambda b, t: (b, 0, 0)),
        pl.BlockSpec((1, 1, D_RG), lambda b, t: (b, 0, 0)),
        pl.BlockSpec((CONV_W, D_RG), lambda b, t: (0, 0)),
        vec,
        pl.BlockSpec((RG_BLOCKS, RG_BW, 2 * RG_BW), lambda b, t: (0, 0, 0)),
        vec, vec, vec,
    ]
    args = [z3, z3, conv0, h0.reshape(n_seq, 1, D_RG), cw, cb.reshape(1, D_RG), wg,
            ba.reshape(1, D_RG), bx.reshape(1, D_RG), lam.reshape(1, D_RG)]
    mix, conv_out, h_out = pl.pallas_call(
        functools.partial(_rglru_kernel, tv=tv, n_t=n_t),
        grid=(n_seq, n_t),
        in_specs=in_specs,
        out_specs=[pl.BlockSpec((tt, D_RG), lambda b, t: (b * sb + t, 0)),
                   pl.BlockSpec((1, CONV_W - 1, D_RG), lambda b, t: (b, 0, 0)),
                   pl.BlockSpec((1, 1, D_RG), lambda b, t: (b, 0, 0))],
        out_shape=[jax.ShapeDtypeStruct((n_seq * seq_stride, D_RG), MXU_DTYPE),
                   jax.ShapeDtypeStruct((n_seq, CONV_W - 1, D_RG), jnp.float32),
                   jax.ShapeDtypeStruct((n_seq, 1, D_RG), jnp.float32)],
        scratch_shapes=[pltpu.VMEM((tt + SUBLANES, D_RG), jnp.float32),
                        pltpu.VMEM((tt, D_RG), jnp.float32),
                        pltpu.VMEM((tt, D_RG), jnp.float32),
                        pltpu.VMEM((1, D_RG), jnp.float32)],
        compiler_params=_cparams(("parallel", "arbitrary")),
        name=name,
    )(*args)
    return mix, conv_out, h_out.reshape(n_seq, D_RG)


def _mlstm_kernel(q_ref, k_ref, v_ref, o_ref, g_ref, c0_ref, n0_ref, m0_ref, ng_ref,
                  mix_ref, co_ref, no_ref, mo_ref, c_s, n_s, m_s, *, lc_last, tv, n_c):
    c = pl.program_id(1)
    ll = mix_ref.shape[0]
    h_n = ML_HEADS

    @pl.when(c == 0)
    def _():
        c_s[...] = c0_ref[0]
        n_s[...] = n0_ref[0]
        m_s[...] = m0_ref[0]

    dn_t = (((1,), (1,)), ((), ()))
    dn_c0 = (((0,), (0,)), ((), ()))

    def run(lc):
        g = g_ref[...]
        row = lax.broadcasted_iota(jnp.int32, (ll, LANES), 0)
        lf = jax.nn.log_sigmoid(g)
        if tv < ll:
            lane = lax.broadcasted_iota(jnp.int32, (ll, LANES), 1)
            g = jnp.where((row >= tv) & (lane < h_n), -jnp.inf, g)
            lf = jnp.where(row >= tv, 0.0, lf)
        rowc = row % lc
        bcum = lf
        s = 1
        while s < lc:
            bcum = bcum + jnp.where(rowc >= s, pltpu.roll(bcum, s, 0), 0.0)
            s *= 2
        g_t = g.T
        b_t = bcum.T
        ti = lax.broadcasted_iota(jnp.int32, (lc, lc), 0)
        si = lax.broadcasted_iota(jnp.int32, (lc, lc), 1)
        causal = si <= ti

        m_val = [m_s[:, h:h + 1] for h in range(h_n)]
        n_val = [n_s[h:h + 1, :] for h in range(h_n)]
        for j in range(ll // lc):
            rs = slice(j * lc, (j + 1) * lc)
            for h in range(h_n):
                cs = slice(h * ML_DH, (h + 1) * ML_DH)
                qf = q_ref[rs, cs]
                kf = k_ref[rs, cs] * (ML_DH ** -0.5)
                vf = v_ref[rs, cs]
                qb, kb, vb = qf.astype(MXU_DTYPE), kf.astype(MXU_DTYPE), vf.astype(MXU_DTYPE)
                ig_col = g[rs, h:h + 1]
                b_col = bcum[rs, h_n + h:h_n + h + 1]
                ig_row = g_t[h:h + 1, rs]
                b_row = b_t[h_n + h:h_n + h + 1, rs]
                m_prev = m_val[h]

                logw = jnp.where(causal, b_col - b_row + ig_row, -jnp.inf)
                gq = b_col + m_prev
                mt = jnp.maximum(gq, jnp.max(logw, axis=-1, keepdims=True))
                w_intra = jnp.exp(logw - mt)
                w_state = jnp.exp(gq - mt)
                s_mat = lax.dot_general(qb, kb, dn_t, preferred_element_type=jnp.float32) * w_intra
                c_prev = c_s[h]
                n_prev = n_val[h]
                num = (w_state * jnp.dot(qb, c_prev.astype(MXU_DTYPE), preferred_element_type=jnp.float32)
                       + jnp.dot(s_mat.astype(MXU_DTYPE), vb, preferred_element_type=jnp.float32))
                den = (w_state * jnp.sum(qf * n_prev, axis=-1, keepdims=True)
                       + jnp.sum(s_mat, axis=-1, keepdims=True))
                hh = num / jnp.maximum(jnp.abs(den), jnp.exp(-mt))

                b_last = b_col[lc - 1:lc, :]
                logu_row = b_last - b_row + ig_row
                m_new = jnp.maximum(b_last + m_prev, jnp.max(logu_row, axis=-1, keepdims=True))
                u_col = jnp.exp(b_last - b_col + ig_col - m_new)
                decay = jnp.exp(b_last + m_prev - m_new)
                ku = kf * u_col
                c_s[h] = decay * c_prev + lax.dot_general(ku.astype(MXU_DTYPE), vb, dn_c0,
                                                          preferred_element_type=jnp.float32)
                n_val[h] = decay * n_prev + jnp.sum(ku, axis=0, keepdims=True)
                m_val[h] = m_new

                hm = hh * lax.rsqrt(jnp.mean(hh * hh, axis=-1, keepdims=True) + RMS_EPS) * ng_ref[:, cs]
                mix_ref[rs, cs] = (hm * jax.nn.sigmoid(o_ref[rs, cs])).astype(mix_ref.dtype)

        for h in range(h_n):
            m_s[:, h:h + 1] = m_val[h]
            n_s[h:h + 1, :] = n_val[h]

    if lc_last == ll:
        run(ll)
    else:
        @pl.when(c < n_c - 1)
        def _():
            run(ll)

        @pl.when(c == n_c - 1)
        def _():
            run(lc_last)

    @pl.when(c == n_c - 1)
    def _():
        co_ref[0] = c_s[...]
        no_ref[0] = n_s[...]
        mo_ref[0] = m_s[...]


def _mlstm_call(z3, gates, c0, n0, m0, ng, *, row0, seq_stride, ll, lc_last, tv, n_c, name):
    n_seq = c0.shape[0]
    blk0, sb = row0 // ll, seq_stride // ll

    def rows(b, c):
        return blk0 + b * sb + c

    def zspec(j):
        return pl.BlockSpec((None, ll, D_ML), lambda b, c: (j, rows(b, c), 0))

    in_specs = [
        zspec(2), zspec(3), zspec(4), zspec(5),
        pl.BlockSpec((ll, LANES), lambda b, c: (rows(b, c), 0)),
        pl.BlockSpec((1, ML_HEADS, ML_DH, ML_DH), lambda b, c: (b, 0, 0, 0)),
        pl.BlockSpec((1, ML_HEADS, ML_DH), lambda b, c: (b, 0, 0)),
        pl.BlockSpec((1, 1, ML_HEADS), lambda b, c: (b, 0, 0)),
        pl.BlockSpec((1, D_ML), lambda b, c: (0, 0)),
    ]
    args = [z3, z3, z3, z3, gates, c0, n0, m0.reshape(n_seq, 1, ML_HEADS), ng.reshape(1, D_ML)]
    mix, c_out, n_out, m_out = pl.pallas_call(
        functools.partial(_mlstm_kernel, lc_last=lc_last, tv=tv, n_c=n_c),
        grid=(n_seq, n_c),
        in_specs=in_specs,
        out_specs=[pl.BlockSpec((ll, D_ML), lambda b, c: (b * sb + c, 0)),
                   pl.BlockSpec((1, ML_HEADS, ML_DH, ML_DH), lambda b, c: (b, 0, 0, 0)),
                   pl.BlockSpec((1, ML_HEADS, ML_DH), lambda b, c: (b, 0, 0)),
                   pl.BlockSpec((1, 1, ML_HEADS), lambda b, c: (b, 0, 0))],
        out_shape=[jax.ShapeDtypeStruct((n_seq * seq_stride, D_ML), MXU_DTYPE),
                   jax.ShapeDtypeStruct((n_seq, ML_HEADS, ML_DH, ML_DH), jnp.float32),
                   jax.ShapeDtypeStruct((n_seq, ML_HEADS, ML_DH), jnp.float32),
                   jax.ShapeDtypeStruct((n_seq, 1, ML_HEADS), jnp.float32)],
        scratch_shapes=[pltpu.VMEM((ML_HEADS, ML_DH, ML_DH), jnp.float32),
                        pltpu.VMEM((ML_HEADS, ML_DH), jnp.float32),
                        pltpu.VMEM((1, ML_HEADS), jnp.float32)],
        compiler_params=_cparams(("parallel", "arbitrary")),
        name=name,
    )(*args)
    return mix, c_out, n_out, m_out.reshape(n_seq, ML_HEADS)


def _first_argmax(vals):
    best, idx = vals[0], jnp.zeros(vals[0].shape, jnp.int32)
    for j in range(1, len(vals)):
        better = vals[j] > best
        best = jnp.where(better, vals[j], best)
        idx = jnp.where(better, j, idx)
    return best, idx


def _pick(vals, idx):
    out = vals[0]
    for j in range(1, len(vals)):
        out = jnp.where(idx == j, vals[j], out)
    return out


def _outproj_kernel(x_ref, rgp_ref, mlp_ref, rgs_ref, mls_ref, wo_ref, g_ref, b_ref, rwt_ref, rb_ref,
                    x1_ref, x1b_ref, e_ref, gc_ref, *, n_prompt_tiles):
    tm = x_ref.shape[0]
    is_prompt = pl.program_id(0) < n_prompt_tiles
    mix_rg = jnp.where(is_prompt, rgp_ref[...], rgs_ref[...])
    mix_ml = jnp.where(is_prompt, mlp_ref[...], mls_ref[...])
    y = (jnp.dot(mix_rg, wo_ref[0:D_RG, :], preferred_element_type=jnp.float32)
         + jnp.dot(mix_ml, wo_ref[D_RG:, :], preferred_element_type=jnp.float32))
    x1 = _layer_norm(ALPHA * x_ref[...] + y, g_ref[...], b_ref[...])
    x1_ref[...] = x1
    x1b = x1.astype(x1b_ref.dtype)
    x1b_ref[...] = x1b

    logits = lax.dot_general(rwt_ref[...], x1b, (((1,), (1,)), ((), ())),
                             preferred_element_type=jnp.float32)
    ex = jnp.exp(logits - jnp.max(logits, axis=0, keepdims=True))
    probs = ex / jnp.sum(ex, axis=0, keepdims=True)
    sel = probs + rb_ref[...]
    sel_r = [sel[e:e + 1, :] for e in range(N_EXPERTS)]
    prob_r = [probs[e:e + 1, :] for e in range(N_EXPERTS)]
    scores = []
    for gi in range(N_GROUPS):
        v = sel_r[gi * EXPERTS_PER_GROUP:(gi + 1) * EXPERTS_PER_GROUP]
        top2 = None
        for i in range(EXPERTS_PER_GROUP):
            for j in range(i + 1, EXPERTS_PER_GROUP):
                pair = v[i] + v[j]
                top2 = pair if top2 is None else jnp.maximum(top2, pair)
        scores.append(top2)
    _, g_idx = _first_argmax(scores)
    in_sel = [_pick([sel_r[gi * EXPERTS_PER_GROUP + j] for gi in range(N_GROUPS)], g_idx)
              for j in range(EXPERTS_PER_GROUP)]
    in_prob = [_pick([prob_r[gi * EXPERTS_PER_GROUP + j] for gi in range(N_GROUPS)], g_idx)
               for j in range(EXPERTS_PER_GROUP)]
    _, loc0 = _first_argmax(in_sel)
    masked = [jnp.where(loc0 == j, -jnp.inf, in_sel[j]) for j in range(EXPERTS_PER_GROUP)]
    _, loc1 = _first_argmax(masked)
    p0, p1 = _pick(in_prob, loc0), _pick(in_prob, loc1)
    psum = p0 + p1
    e0 = g_idx * EXPERTS_PER_GROUP + loc0
    e1 = g_idx * EXPERTS_PER_GROUP + loc1
    e_ref[...] = jnp.concatenate([e0, e1, jnp.zeros((SUBLANES - TOP_K, tm), jnp.int32)], axis=0)
    gates_t = jnp.concatenate([p0 / psum, p1 / psum, jnp.zeros((LANES - TOP_K, tm), jnp.float32)], axis=0)
    gc_ref[...] = gates_t.T


def _outproj_call(x, rg_p, ml_p, rg_s, ml_s, wo, g, b, rwt, rb):
    n, d = x.shape
    tm = TM_OUT
    npt = rg_p.shape[0] // tm
    row = pl.BlockSpec((tm, d), lambda i: (i, 0))
    half_p = pl.BlockSpec((tm, D_RG), lambda i: (jnp.minimum(i, npt - 1), 0))
    half_s = pl.BlockSpec((tm, D_RG), lambda i: (jnp.maximum(i - npt, 0), 0))
    vec = pl.BlockSpec((1, d), lambda i: (0, 0))
    return pl.pallas_call(
        functools.partial(_outproj_kernel, n_prompt_tiles=npt),
        grid=(n // tm,),
        in_specs=[row, half_p, half_p, half_s, half_s,
                  pl.BlockSpec((d, d), lambda i: (0, 0), pipeline_mode=pl.Buffered(1)),
                  vec, vec,
                  pl.BlockSpec((N_EXPERTS, d), lambda i: (0, 0)),
                  pl.BlockSpec((N_EXPERTS, 1), lambda i: (0, 0))],
        out_specs=[row, row,
                   pl.BlockSpec((SUBLANES, tm), lambda i: (0, i)),
                   pl.BlockSpec((tm, LANES), lambda i: (i, 0))],
        out_shape=[jax.ShapeDtypeStruct((n, d), jnp.float32),
                   jax.ShapeDtypeStruct((n, d), MXU_DTYPE),
                   jax.ShapeDtypeStruct((SUBLANES, n), jnp.int32),
                   jax.ShapeDtypeStruct((n, LANES), jnp.float32)],
        compiler_params=_cparams(("parallel",)),
        name="outproj_ln_router",
    )(x, rg_p, ml_p, rg_s, ml_s, wo, g.reshape(1, d), b.reshape(1, d), rwt, rb.reshape(N_EXPERTS, 1))


def _dispatch_plan(e_t, tb, nb):
    n = e_t.shape[1]
    p = TOP_K * n
    i32 = jnp.int32
    flat_e = e_t.T.reshape(p)
    sk = jnp.sort(flat_e * p + jnp.arange(p, dtype=i32))
    order = sk % p
    counts = jnp.sum((flat_e[:, None] == jnp.arange(N_EXPERTS, dtype=i32)[None, :]).astype(i32), axis=0)
    start = jnp.cumsum(counts) - counts
    pcounts = (counts + tb - 1) // tb * tb
    pend = jnp.cumsum(pcounts)
    pstart = pend - pcounts
    r = jnp.arange(nb * tb, dtype=i32)
    er = jnp.minimum(jnp.sum((r[:, None] >= pend[None, :]).astype(i32), axis=1), N_EXPERTS - 1)
    off = r - pstart[er]
    real = (r < pend[-1]) & (off < counts[er])
    n_real_before = jnp.where(r < pend[-1], start[er] + jnp.minimum(off, counts[er]), p)
    pair = order[jnp.clip(start[er] + off, 0, p - 1)]
    tok, k = pair // TOP_K, pair % TOP_K
    src_tok = jnp.where(real, tok, 0).astype(i32)
    dst_row = jnp.where(real, k * n + tok, p + (r - n_real_before)).astype(i32)
    blk_e = er[::tb]
    return src_tok, dst_row, blk_e


def _moe_kernel(be_ref, src_ref, srcn_ref, dst_ref, dstp_ref, x_hbm, w1_ref, w3_ref, w2_ref, y_hbm,
                xbuf, ybuf, xb_ref, gsem, ssem):
    r = pl.program_id(0)
    n_r = pl.num_programs(0)
    tb = ybuf.shape[1]
    slot = r % 2
    other = 1 - slot

    def gather_row(idx_ref, s, j):
        pltpu.make_async_copy(x_hbm.at[pl.ds(idx_ref[j], 1), :],
                              xbuf.at[s, pl.ds(j, 1), :], gsem.at[s]).start()

    def scatter_row(idx_ref, s, j):
        pltpu.make_async_copy(ybuf.at[s, pl.ds(j, 1), :],
                              y_hbm.at[pl.ds(idx_ref[j], 1), :], ssem.at[s]).start()

    def wait_gather(s):
        pltpu.make_async_copy(x_hbm.at[pl.ds(0, tb), :], xbuf.at[s], gsem.at[s]).wait()

    def wait_scatter(s):
        pltpu.make_async_copy(ybuf.at[s], y_hbm.at[pl.ds(0, tb), :], ssem.at[s]).wait()

    def stage(s):
        xb_ref[...] = xbuf[s].astype(MXU_DTYPE)

    def expert(s):
        xb = xb_ref[...]
        h1 = jnp.dot(xb, w1_ref[0], preferred_element_type=jnp.float32)
        h3 = jnp.dot(xb, w3_ref[0], preferred_element_type=jnp.float32)
        hb = (jax.nn.silu(h1) * h3).astype(MXU_DTYPE)
        ybuf[s] = jnp.dot(hb, w2_ref[0], preferred_element_type=jnp.float32)

    @pl.when(r == 0)
    def _():
        for j in range(tb):
            gather_row(src_ref, 0, j)
        wait_gather(0)
        stage(0)
        for j in range(tb):
            gather_row(srcn_ref, 1, j)
        expert(0)

    @pl.when(r > 0)
    def _():
        wait_gather(slot)
        stage(slot)
        for j in range(tb):
            gather_row(srcn_ref, other, j)
            scatter_row(dstp_ref, other, j)
        expert(slot)
        wait_scatter(other)

    @pl.when(r == n_r - 1)
    def _():
        wait_gather(other)

        def put(j, carry):
            scatter_row(dst_ref, slot, j)
            return carry
        lax.fori_loop(0, tb, put, 0)
        wait_scatter(slot)


def _moe_call(x1, src_tok, dst_row, blk_e, w1, w3, w2, tb, nb):
    n, d = x1.shape
    f = w1.shape[-1]
    assert nb >= 2

    def wblk(r, be):
        return (be[r], 0, 0)

    grid_spec = pltpu.PrefetchScalarGridSpec(
        num_scalar_prefetch=1,
        grid=(nb,),
        in_specs=[
            pl.BlockSpec((tb,), lambda r, be: (r,), memory_space=pltpu.SMEM),
            pl.BlockSpec((tb,), lambda r, be: (jnp.minimum(r + 1, nb - 1),), memory_space=pltpu.SMEM),
            pl.BlockSpec((tb,), lambda r, be: (r,), memory_space=pltpu.SMEM),
            pl.BlockSpec((tb,), lambda r, be: (jnp.maximum(r - 1, 0),), memory_space=pltpu.SMEM),
            pl.BlockSpec(memory_space=pl.ANY),
            pl.BlockSpec((1, d, f), wblk),
            pl.BlockSpec((1, d, f), wblk),
            pl.BlockSpec((1, f, d), wblk),
        ],
        out_specs=pl.BlockSpec(memory_space=pl.ANY),
        scratch_shapes=[pltpu.VMEM((2, tb, d), jnp.float32), pltpu.VMEM((2, tb, d), jnp.float32),
                        pltpu.VMEM((tb, d), MXU_DTYPE),
                        pltpu.SemaphoreType.DMA((2,)), pltpu.SemaphoreType.DMA((2,))],
    )
    return pl.pallas_call(
        _moe_kernel,
        grid_spec=grid_spec,
        out_shape=jax.ShapeDtypeStruct((nb * tb, d), jnp.float32),
        compiler_params=_cparams(("arbitrary",)),
        name="moe_experts",
    )(blk_e, src_tok, src_tok, dst_row, dst_row, x1, w1, w3, w2)


def _combine_math(x_ref, y0_ref, y1_ref, gc_ref, g_ref, b_ref):
    gc = gc_ref[...]
    f = y0_ref[...] * gc[:, 0:1] + y1_ref[...] * gc[:, 1:2]
    return _layer_norm(ALPHA * x_ref[...] + f, g_ref[...], b_ref[...])


def _combine_kernel(x_ref, y0_ref, y1_ref, gc_ref, g_ref, b_ref, o_ref, ob_ref):
    y = _combine_math(x_ref, y0_ref, y1_ref, gc_ref, g_ref, b_ref)
    o_ref[...] = y
    ob_ref[...] = y.astype(ob_ref.dtype)


def _combine_last_kernel(x_ref, y0_ref, y1_ref, gc_ref, g_ref, b_ref, op_ref, os_ref, *, n_prompt_tiles):
    y = _combine_math(x_ref, y0_ref, y1_ref, gc_ref, g_ref, b_ref)
    is_prompt = pl.program_id(0) < n_prompt_tiles

    @pl.when(is_prompt)
    def _():
        op_ref[...] = y

    @pl.when(jnp.logical_not(is_prompt))
    def _():
        os_ref[...] = y


def _combine_call(x1, gcol, ys, g, b, n_p=None):
    n, d = x1.shape
    tm = TM_CMB
    nt = n // tm
    row = pl.BlockSpec((tm, d), lambda i: (i, 0))
    vec = pl.BlockSpec((1, d), lambda i: (0, 0))
    in_specs = [row, row, pl.BlockSpec((tm, d), lambda i: (nt + i, 0)),
                pl.BlockSpec((tm, LANES), lambda i: (i, 0)), vec, vec]
    args = (x1, ys, ys, gcol, g.reshape(1, d), b.reshape(1, d))
    if n_p is None:
        return pl.pallas_call(
            _combine_kernel,
            grid=(nt,),
            in_specs=in_specs,
            out_specs=[row, row],
            out_shape=[jax.ShapeDtypeStruct((n, d), jnp.float32),
                       jax.ShapeDtypeStruct((n, d), MXU_DTYPE)],
            compiler_params=_cparams(("parallel",)),
            name="combine_ln",
        )(*args)
    npt = n_p // tm
    return pl.pallas_call(
        functools.partial(_combine_last_kernel, n_prompt_tiles=npt),
        grid=(nt,),
        in_specs=in_specs,
        out_specs=[pl.BlockSpec((tm, d), lambda i: (jnp.minimum(i, npt - 1), 0)),
                   pl.BlockSpec((tm, d), lambda i: (jnp.maximum(i - npt, 0), 0))],
        out_shape=[jax.ShapeDtypeStruct((n_p, d), jnp.float32),
                   jax.ShapeDtypeStruct((n - n_p, d), jnp.float32)],
        compiler_params=_cparams(("arbitrary",)),
        name="combine_ln_last",
    )(*args)


def kernel(x_prompt, x_sample, state_conv, state_lru, state_mlstm_C, state_mlstm_n, state_mlstm_m,
           ln_in_g, ln_in_b, w_in, b_in, conv_w, conv_b, lru_wa, lru_ba, lru_wx, lru_bx, lru_lambda,
           mlstm_norm_g, w_out, ln1_g, ln1_b, router_w, router_b, w1, w3, w2, ln2_g, ln2_b):
    bp, tp, d = x_prompt.shape
    bs, ts, _ = x_sample.shape
    depth = w_in.shape[0]
    n_p = bp * tp
    n_s = bs * SLOT
    n_pad = n_p + n_s
    assert d == D_MODEL and ts <= min(SLOT, CHUNK) and ts % SUBLANES == 0
    assert tp % T_RG == 0 and tp % L_ML == 0
    assert n_p % TM_PROJ == 0 and n_s % TM_PROJ == 0
    f32 = jnp.float32

    nb = -(-(TOP_K * n_pad + N_EXPERTS * (TB_MOE - 1)) // TB_MOE)

    w_gate_m = jnp.pad(w_in[:, :, GATE_OFF:], ((0, 0), (0, 0), (0, LANES - 2 * ML_HEADS))).astype(MXU_DTYPE)
    b_gate = jnp.pad(b_in[:, GATE_OFF:], ((0, 0), (0, LANES - 2 * ML_HEADS)))
    w_rg_m = jnp.concatenate([lru_wa, lru_wx], axis=-1).astype(MXU_DTYPE)
    w_out_m = w_out.astype(MXU_DTYPE)
    rwt_m = router_w.T.astype(MXU_DTYPE)
    w1_m, w3_m, w2_m = w1.astype(MXU_DTYPE), w3.astype(MXU_DTYPE), w2.astype(MXU_DTYPE)

    zp = lambda *s: jnp.zeros(s, f32)
    x, xb = _ln_call(x_prompt.reshape(n_p, d),
                     jnp.pad(x_sample, ((0, 0), (0, SLOT - ts), (0, 0))).reshape(n_s, d), ln_in_g, ln_in_b)
    outs = {k: [] for k in ("pc", "pl", "pC", "pn", "pm", "sc", "sl", "sC", "sn", "sm")}
    for l in range(depth):
        z3 = _in_proj_call(xb, w_in, b_in, l)
        gates = _gate_proj_call(xb, w_gate_m[l], b_gate[l].reshape(1, LANES))

        rg_w = (conv_w[l], conv_b[l], w_rg_m[l], lru_ba[l], lru_bx[l], lru_lambda[l])
        rg_p, pc, plru = _rglru_call(z3, zp(bp, CONV_W - 1, D_RG), zp(bp, D_RG), *rg_w, row0=0, seq_stride=tp,
                                     tt=T_RG, tv=T_RG, n_t=tp // T_RG, name="rglru_prompt")
        rg_s, sc, slru = _rglru_call(z3, state_conv[l], state_lru[l], *rg_w, row0=n_p, seq_stride=SLOT,
                                     tt=SLOT, tv=ts, n_t=1, name="rglru_sample")
        ml_p, pC, pn, pm = _mlstm_call(z3, gates, zp(bp, ML_HEADS, ML_DH, ML_DH), zp(bp, ML_HEADS, ML_DH),
                                       zp(bp, ML_HEADS), mlstm_norm_g[l], row0=0, seq_stride=tp,
                                       ll=L_ML, lc_last=min(CHUNK, tp), tv=L_ML, n_c=tp // L_ML, name="mlstm_prompt")
        ml_s, sC, sn, sm = _mlstm_call(z3, gates, state_mlstm_C[l], state_mlstm_n[l], state_mlstm_m[l],
                                       mlstm_norm_g[l], row0=n_p, seq_stride=SLOT,
                                       ll=SLOT, lc_last=SLOT, tv=ts, n_c=1, name="mlstm_sample")

        x1, x1b, e_t, gcol = _outproj_call(x, rg_p, ml_p, rg_s, ml_s, w_out_m[l], ln1_g[l], ln1_b[l],
                                           rwt_m, router_b)
        src_tok, dst_row, blk_e = _dispatch_plan(e_t[:TOP_K], TB_MOE, nb)
        ys = _moe_call(x1, src_tok, dst_row, blk_e, w1_m[l], w3_m[l], w2_m[l], TB_MOE, nb)
        if l + 1 < depth:
            x, xb = _combine_call(x1, gcol, ys, ln2_g[l], ln2_b[l])
        else:
            y_p, y_s = _combine_call(x1, gcol, ys, ln2_g[l], ln2_b[l], n_p=n_p)

        for k, v in zip(("pc", "pl", "pC", "pn", "pm", "sc", "sl", "sC", "sn", "sm"),
                        (pc, plru, pC, pn, pm, sc, slru, sC, sn, sm)):
            outs[k].append(v)

    y_prompt = y_p.reshape(bp, tp, d)
    y_sample = y_s.reshape(bs, SLOT, d)[:, :ts]
    st = {k: jnp.stack(v) for k, v in outs.items()}
    return (y_prompt, y_sample, st["pc"], st["pl"], st["pC"], st["pn"], st["pm"],
            st["sc"], st["sl"], st["sC"], st["sn"], st["sm"])
```

```python
import functools

import jax
import jax.numpy as jnp
from jax import lax
from jax.experimental import pallas as pl
from jax.experimental.pallas import tpu as pltpu

DEPTH = 4
D_MODEL = 2048
D_RG = 1024
RG_BLOCKS = 8
RG_BW = D_RG // RG_BLOCKS
CONV_W = 4
LRU_C = 8.0
D_ML = 1024
ML_HEADS = 4
ML_DH = D_ML // ML_HEADS
CHUNK = 64
N_EXPERTS = 16
N_GROUPS = 4
EXPERTS_PER_GROUP = N_EXPERTS // N_GROUPS
TOP_K = 2
D_FF = D_MODEL // 2
ALPHA = (2 * DEPTH) ** 0.25
LN_EPS = 1e-5
RMS_EPS = 1e-6
N_PROJ = 6
GATE_OFF = N_PROJ * D_RG

LANES = 128
SUBLANES = 8
MXU_DTYPE = jnp.bfloat16
SLOT = 128
TM_PROJ = 1024
TM_LN = 512
TM_OUT = 512
TM_CMB = 512
TB_MOE = 256
T_RG = 512
L_ML = 256
VMEM_LIMIT = 56 * 1024 * 1024


def _cparams(sem):
    return pltpu.CompilerParams(dimension_semantics=sem, vmem_limit_bytes=VMEM_LIMIT)


def _layer_norm(h, g, b):
    mu = jnp.mean(h, axis=-1, keepdims=True)
    hc = h - mu
    var = jnp.mean(hc * hc, axis=-1, keepdims=True)
    return hc * lax.rsqrt(var + LN_EPS) * g + b


def _ln_kernel(xp_ref, xs_ref, g_ref, b_ref, o_ref, ob_ref, *, n_prompt_tiles):
    x = jnp.where(pl.program_id(0) < n_prompt_tiles, xp_ref[...], xs_ref[...])
    y = _layer_norm(x, g_ref[...], b_ref[...])
    o_ref[...] = y
    ob_ref[...] = y.astype(ob_ref.dtype)


def _ln_call(xp, xs, g, b):
    (n_p, d), n_s = xp.shape, xs.shape[0]
    tm = TM_LN
    npt = n_p // tm
    n = n_p + n_s
    row = pl.BlockSpec((tm, d), lambda i: (i, 0))
    vec = pl.BlockSpec((1, d), lambda i: (0, 0))
    return pl.pallas_call(
        functools.partial(_ln_kernel, n_prompt_tiles=npt),
        grid=(n // tm,),
        in_specs=[pl.BlockSpec((tm, d), lambda i: (jnp.minimum(i, npt - 1), 0)),
                  pl.BlockSpec((tm, d), lambda i: (jnp.maximum(i - npt, 0), 0)),
                  vec, vec],
        out_specs=[row, row],
        out_shape=[jax.ShapeDtypeStruct((n, d), jnp.float32),
                   jax.ShapeDtypeStruct((n, d), MXU_DTYPE)],
        compiler_params=_cparams(("parallel",)),
        name="ln_in",
    )(xp, xs, g.reshape(1, d), b.reshape(1, d))


def _proj_kernel(x_ref, w_ref, b_ref, o_ref):
    o_ref[...] = jnp.dot(x_ref[...], w_ref[...], preferred_element_type=jnp.float32) + b_ref[...]


def _in_proj_call(xb, w_in_m, b_in, layer):
    n, d = xb.shape
    depth, d_in = b_in.shape
    return pl.pallas_call(
        _proj_kernel,
        grid=(N_PROJ, n // TM_PROJ),
        in_specs=[pl.BlockSpec((TM_PROJ, d), lambda j, i: (i, 0)),
                  pl.BlockSpec((None, d, D_RG), lambda j, i: (layer, 0, j)),
                  pl.BlockSpec((None, 1, D_RG), lambda j, i: (layer, 0, j))],
        out_specs=pl.BlockSpec((None, TM_PROJ, D_RG), lambda j, i: (j, i, 0)),
        out_shape=jax.ShapeDtypeStruct((N_PROJ, n, D_RG), jnp.float32),
        compiler_params=_cparams(("parallel", "parallel")),
        name="in_proj",
    )(xb, w_in_m, b_in.reshape(depth, 1, d_in))


def _gate_proj_call(xb, wg, bg):
    n, d = xb.shape
    return pl.pallas_call(
        _proj_kernel,
        grid=(n // TM_PROJ,),
        in_specs=[pl.BlockSpec((TM_PROJ, d), lambda i: (i, 0)),
                  pl.BlockSpec((d, LANES), lambda i: (0, 0)),
                  pl.BlockSpec((1, LANES), lambda i: (0, 0))],
        out_specs=pl.BlockSpec((TM_PROJ, LANES), lambda i: (i, 0)),
        out_shape=jax.ShapeDtypeStruct((n, LANES), jnp.float32),
        compiler_params=_cparams(("parallel",)),
        name="gate_proj",
    )(xb, wg, bg)


def _rglru_kernel(xr_ref, gr_ref, cs_ref, h0_ref, cw_ref, cb_ref, wg_ref, ba_ref, bx_ref, lam_ref,
                  mix_ref, cso_ref, ho_ref, ext_ref, a_ref, u_ref, hc_ref, *, tv, n_t):
    t = pl.program_id(1)
    tt = mix_ref.shape[0]

    @pl.when(t == 0)
    def _():
        ext_ref[0:SUBLANES, :] = jnp.zeros((SUBLANES, D_RG), jnp.float32)
        ext_ref[SUBLANES - (CONV_W - 1):SUBLANES, :] = cs_ref[0]
        hc_ref[...] = h0_ref[0]

    u_in = xr_ref[0:tv, :]
    ext_ref[SUBLANES:SUBLANES + tv, :] = u_in
    cw = cw_ref[...]
    xc = cb_ref[...]
    for tap in range(CONV_W - 1):
        off = SUBLANES - (CONV_W - 1) + tap
        xc = xc + ext_ref[off:off + tv, :] * cw[tap:tap + 1, :]
    xc = xc + u_in * cw[CONV_W - 1:CONV_W, :]

    @pl.when(t == n_t - 1)
    def _():
        cso_ref[0] = ext_ref[SUBLANES + tv - (CONV_W - 1):SUBLANES + tv, :]

    ext_ref[0:SUBLANES, :] = ext_ref[tv:tv + SUBLANES, :]

    xcb = xc.astype(MXU_DTYPE)
    rowmod = lax.broadcasted_iota(jnp.int32, (tv, RG_BW), 0) % SUBLANES
    for nb in range(RG_BLOCKS):
        cs = slice(nb * RG_BW, (nb + 1) * RG_BW)
        pre = jnp.dot(xcb[:, cs], wg_ref[nb], preferred_element_type=jnp.float32)
        r = jax.nn.sigmoid(pre[:, :RG_BW] + ba_ref[:, cs])
        ig = jax.nn.sigmoid(pre[:, RG_BW:] + bx_ref[:, cs])
        log_a = LRU_C * r * jax.nn.log_sigmoid(lam_ref[:, cs])
        a = jnp.exp(log_a)
        th = jnp.tanh(log_a)
        u = jnp.sqrt(-2.0 * th / (1.0 - th)) * (ig * xc[:, cs])
        for s in (1, 2, 4):
            a_s = pltpu.roll(a, s, 0)
            u_s = pltpu.roll(u, s, 0)
            m = rowmod >= s
            u = jnp.where(m, a * u_s + u, u)
            a = jnp.where(m, a * a_s, a)
        a_ref[0:tv, cs] = a
        u_ref[0:tv, cs] = u

    def body(g, carry):
        rows = pl.ds(pl.multiple_of(g * SUBLANES, SUBLANES), SUBLANES)
        h = a_ref[rows, :] * carry + u_ref[rows, :]
        u_ref[rows, :] = h
        return h[SUBLANES - 1:SUBLANES, :]

    h_last = lax.fori_loop(0, tv // SUBLANES, body, hc_ref[...])
    hc_ref[...] = h_last
    ho_ref[0] = h_last

    mix_ref[0:tv, :] = (jax.nn.gelu(gr_ref[0:tv, :]) * u_ref[0:tv, :]).astype(mix_ref.dtype)
    if tv < tt:
        mix_ref[tv:tt, :] = jnp.zeros((tt - tv, D_RG), mix_ref.dtype)


def _rglru_call(z3, conv0, h0, cw, cb, wg, ba, bx, lam, *, row0, seq_stride, tt, tv, n_t, name):
    n_seq = conv0.shape[0]
    blk0, sb = row0 // tt, seq_stride // tt

    def rows(b, t):
        return blk0 + b * sb + t

    vec = pl.BlockSpec((1, D_RG), lambda b, t: (0, 0))
    in_specs = [
        pl.BlockSpec((None, tt, D_RG), lambda b, t: (0, rows(b, t), 0)),
        pl.BlockSpec((None, tt, D_RG), lambda b, t: (1, rows(b, t), 0)),
        pl.BlockSpec((1, CONV_W - 1, D_RG), lambda b, t: (b, 0, 0)),
        pl.BlockSpec((1, 1, D_RG), lambda b, t: (b, 0, 0)),
        pl.BlockSpec((CONV_W, D_RG), lambda b, t: (0, 0)),
        vec,
        pl.BlockSpec((RG_BLOCKS, RG_BW, 2 * RG_BW), lambda b, t: (0, 0, 0)),
        vec, vec, vec,
    ]
    args = [z3, z3, conv0, h0.reshape(n_seq, 1, D_RG), cw, cb.reshape(1, D_RG), wg,
            ba.reshape(1, D_RG), bx.reshape(1, D_RG), lam.reshape(1, D_RG)]
    mix, conv_out, h_out = pl.pallas_call(
        functools.partial(_rglru_kernel, tv=tv, n_t=n_t),
        grid=(n_seq, n_t),
        in_specs=in_specs,
        out_specs=[pl.BlockSpec((tt, D_RG), lambda b, t: (b * sb + t, 0)),
                   pl.BlockSpec((1, CONV_W - 1, D_RG), lambda b, t: (b, 0, 0)),
                   pl.BlockSpec((1, 1, D_RG), lambda b, t: (b, 0, 0))],
        out_shape=[jax.ShapeDtypeStruct((n_seq * seq_stride, D_RG), MXU_DTYPE),
                   jax.ShapeDtypeStruct((n_seq, CONV_W - 1, D_RG), jnp.float32),
                   jax.ShapeDtypeStruct((n_seq, 1, D_RG), jnp.float32)],
        scratch_shapes=[pltpu.VMEM((tt + SUBLANES, D_RG), jnp.float32),
                        pltpu.VMEM((tt, D_RG), jnp.float32),
                        pltpu.VMEM((tt, D_RG), jnp.float32),
                        pltpu.VMEM((1, D_RG), jnp.float32)],
        compiler_params=_cparams(("parallel", "arbitrary")),
        name=name,
    )(*args)
    return mix, conv_out, h_out.reshape(n_seq, D_RG)


def _mlstm_kernel(q_ref, k_ref, v_ref, o_ref, g_ref, c0_ref, n0_ref, m0_ref, ng_ref,
                  mix_ref, co_ref, no_ref, mo_ref, c_s, n_s, m_s, *, lc_last, tv, n_c):
    c = pl.program_id(1)
    ll = mix_ref.shape[0]
    h_n = ML_HEADS

    @pl.when(c == 0)
    def _():
        c_s[...] = c0_ref[0]
        n_s[...] = n0_ref[0]
        m_s[...] = m0_ref[0]

    dn_t = (((1,), (1,)), ((), ()))
    dn_c0 = (((0,), (0,)), ((), ()))

    def run(lc):
        g = g_ref[...]
        row = lax.broadcasted_iota(jnp.int32, (ll, LANES), 0)
        lf = jax.nn.log_sigmoid(g)
        if tv < ll:
            lane = lax.broadcasted_iota(jnp.int32, (ll, LANES), 1)
            g = jnp.where((row >= tv) & (lane < h_n), -jnp.inf, g)
            lf = jnp.where(row >= tv, 0.0, lf)
        rowc = row % lc
        bcum = lf
        s = 1
        while s < lc:
            bcum = bcum + jnp.where(rowc >= s, pltpu.roll(bcum, s, 0), 0.0)
            s *= 2
        g_t = g.T
        b_t = bcum.T
        ti = lax.broadcasted_iota(jnp.int32, (lc, lc), 0)
        si = lax.broadcasted_iota(jnp.int32, (lc, lc), 1)
        causal = si <= ti

        m_val = [m_s[:, h:h + 1] for h in range(h_n)]
        n_val = [n_s[h:h + 1, :] for h in range(h_n)]
        for j in range(ll // lc):
            rs = slice(j * lc, (j + 1) * lc)
            for h in range(h_n):
                cs = slice(h * ML_DH, (h + 1) * ML_DH)
                qf = q_ref[rs, cs]
                kf = k_ref[rs, cs] * (ML_DH ** -0.5)
                vf = v_ref[rs, cs]
                qb, kb, vb = qf.astype(MXU_DTYPE), kf.astype(MXU_DTYPE), vf.astype(MXU_DTYPE)
                ig_col = g[rs, h:h + 1]
                b_col = bcum[rs, h_n + h:h_n + h + 1]
                ig_row = g_t[h:h + 1, rs]
                b_row = b_t[h_n + h:h_n + h + 1, rs]
                m_prev = m_val[h]

                logw = jnp.where(causal, b_col - b_row + ig_row, -jnp.inf)
                gq = b_col + m_prev
                mt = jnp.maximum(gq, jnp.max(logw, axis=-1, keepdims=True))
                w_intra = jnp.exp(logw - mt)
                w_state = jnp.exp(gq - mt)
                s_mat = lax.dot_general(qb, kb, dn_t, preferred_element_type=jnp.float32) * w_intra
                c_prev = c_s[h]
                n_prev = n_val[h]
                num = (w_state * jnp.dot(qb, c_prev.astype(MXU_DTYPE), preferred_element_type=jnp.float32)
                       + jnp.dot(s_mat.astype(MXU_DTYPE), vb, preferred_element_type=jnp.float32))
                den = (w_state * jnp.sum(qf * n_prev, axis=-1, keepdims=True)
                       + jnp.sum(s_mat, axis=-1, keepdims=True))
                hh = num / jnp.maximum(jnp.abs(den), jnp.exp(-mt))

                b_last = b_col[lc - 1:lc, :]
                logu_row = b_last - b_row + ig_row
                m_new = jnp.maximum(b_last + m_prev, jnp.max(logu_row, axis=-1, keepdims=True))
                u_col = jnp.exp(b_last - b_col + ig_col - m_new)
                decay = jnp.exp(b_last + m_prev - m_new)
                ku = kf * u_col
                c_s[h] = decay * c_prev + lax.dot_general(ku.astype(MXU_DTYPE), vb, dn_c0,
                                                          preferred_element_type=jnp.float32)
                n_val[h] = decay * n_prev + jnp.sum(ku, axis=0, keepdims=True)
                m_val[h] = m_new

                hm = hh * lax.rsqrt(jnp.mean(hh * hh, axis=-1, keepdims=True) + RMS_EPS) * ng_ref[:, cs]
                mix_ref[rs, cs] = (hm * jax.nn.sigmoid(o_ref[rs, cs])).astype(mix_ref.dtype)

        for h in range(h_n):
            m_s[:, h:h + 1] = m_val[h]
            n_s[h:h + 1, :] = n_val[h]

    if lc_last == ll:
        run(ll)
    else:
        @pl.when(c < n_c - 1)
        def _():
            run(ll)

        @pl.when(c == n_c - 1)
        def _():
            run(lc_last)

    @pl.when(c == n_c - 1)
    def _():
        co_ref[0] = c_s[...]
        no_ref[0] = n_s[...]
        mo_ref[0] = m_s[...]


def _mlstm_call(z3, gates, c0, n0, m0, ng, *, row0, seq_stride, ll, lc_last, tv, n_c, name):
    n_seq = c0.shape[0]
    blk0, sb = row0 // ll, seq_stride // ll

    def rows(b, c):
        return blk0 + b * sb + c

    def zspec(j):
        return pl.BlockSpec((None, ll, D_ML), lambda b, c: (j, rows(b, c), 0))

    in_specs = [
        zspec(2), zspec(3), zspec(4), zspec(5),
        pl.BlockSpec((ll, LANES), lambda b, c: (rows(b, c), 0)),
        pl.BlockSpec((1, ML_HEADS, ML_DH, ML_DH), lambda b, c: (b, 0, 0, 0)),
        pl.BlockSpec((1, ML_HEADS, ML_DH), lambda b, c: (b, 0, 0)),
        pl.BlockSpec((1, 1, ML_HEADS), lambda b, c: (b, 0, 0)),
        pl.BlockSpec((1, D_ML), lambda b, c: (0, 0)),
    ]
    args = [z3, z3, z3, z3, gates, c0, n0, m0.reshape(n_seq, 1, ML_HEADS), ng.reshape(1, D_ML)]
    mix, c_out, n_out, m_out = pl.pallas_call(
        functools.partial(_mlstm_kernel, lc_last=lc_last, tv=tv, n_c=n_c),
        grid=(n_seq, n_c),
        in_specs=in_specs,
        out_specs=[pl.BlockSpec((ll, D_ML), lambda b, c: (b * sb + c, 0)),
                   pl.BlockSpec((1, ML_HEADS, ML_DH, ML_DH), lambda b, c: (b, 0, 0, 0)),
                   pl.BlockSpec((1, ML_HEADS, ML_DH), lambda b, c: (b, 0, 0)),
                   pl.BlockSpec((1, 1, ML_HEADS), lambda b, c: (b, 0, 0))],
        out_shape=[jax.ShapeDtypeStruct((n_seq * seq_stride, D_ML), MXU_DTYPE),
                   jax.ShapeDtypeStruct((n_seq, ML_HEADS, ML_DH, ML_DH), jnp.float32),
                   jax.ShapeDtypeStruct((n_seq, ML_HEADS, ML_DH), jnp.float32),
                   jax.ShapeDtypeStruct((n_seq, 1, ML_HEADS), jnp.float32)],
        scratch_shapes=[pltpu.VMEM((ML_HEADS, ML_DH, ML_DH), jnp.float32),
                        pltpu.VMEM((ML_HEADS, ML_DH), jnp.float32),
                        pltpu.VMEM((1, ML_HEADS), jnp.float32)],
        compiler_params=_cparams(("parallel", "arbitrary")),
        name=name,
    )(*args)
    return mix, c_out, n_out, m_out.reshape(n_seq, ML_HEADS)


def _first_argmax(vals):
    best, idx = vals[0], jnp.zeros(vals[0].shape, jnp.int32)
    for j in range(1, len(vals)):
        better = vals[j] > best
        best = jnp.where(better, vals[j], best)
        idx = jnp.where(better, j, idx)
    return best, idx


def _pick(vals, idx):
    out = vals[0]
    for j in range(1, len(vals)):
        out = jnp.where(idx == j, vals[j], out)
    return out


def _outproj_kernel(x_ref, rgp_ref, mlp_ref, rgs_ref, mls_ref, wo_ref, g_ref, b_ref, rwt_ref, rb_ref,
                    x1_ref, x1b_ref, e_ref, gc_ref, *, n_prompt_tiles):
    tm = x_ref.shape[0]
    is_prompt = pl.program_id(0) < n_prompt_tiles
    mix_rg = jnp.where(is_prompt, rgp_ref[...], rgs_ref[...])
    mix_ml = jnp.where(is_prompt, mlp_ref[...], mls_ref[...])
    y = (jnp.dot(mix_rg, wo_ref[0:D_RG, :], preferred_element_type=jnp.float32)
         + jnp.dot(mix_ml, wo_ref[D_RG:, :], preferred_element_type=jnp.float32))
    x1 = _layer_norm(ALPHA * x_ref[...] + y, g_ref[...], b_ref[...])
    x1_ref[...] = x1
    x1b = x1.astype(x1b_ref.dtype)
    x1b_ref[...] = x1b

    logits = lax.dot_general(rwt_ref[...], x1b, (((1,), (1,)), ((), ())),
                             preferred_element_type=jnp.float32)
    ex = jnp.exp(logits - jnp.max(logits, axis=0, keepdims=True))
    probs = ex / jnp.sum(ex, axis=0, keepdims=True)
    sel = probs + rb_ref[...]
    sel_r = [sel[e:e + 1, :] for e in range(N_EXPERTS)]
    prob_r = [probs[e:e + 1, :] for e in range(N_EXPERTS)]
    scores = []
    for gi in range(N_GROUPS):
        v = sel_r[gi * EXPERTS_PER_GROUP:(gi + 1) * EXPERTS_PER_GROUP]
        top2 = None
        for i in range(EXPERTS_PER_GROUP):
            for j in range(i + 1, EXPERTS_PER_GROUP):
                pair = v[i] + v[j]
                top2 = pair if top2 is None else jnp.maximum(top2, pair)
        scores.append(top2)
    _, g_idx = _first_argmax(scores)
    in_sel = [_pick([sel_r[gi * EXPERTS_PER_GROUP + j] for gi in range(N_GROUPS)], g_idx)
              for j in range(EXPERTS_PER_GROUP)]
    in_prob = [_pick([prob_r[gi * EXPERTS_PER_GROUP + j] for gi in range(N_GROUPS)], g_idx)
               for j in range(EXPERTS_PER_GROUP)]
    _, loc0 = _first_argmax(in_sel)
    masked = [jnp.where(loc0 == j, -jnp.inf, in_sel[j]) for j in range(EXPERTS_PER_GROUP)]
    _, loc1 = _first_argmax(masked)
    p0, p1 = _pick(in_prob, loc0), _pick(in_prob, loc1)
    psum = p0 + p1
    e0 = g_idx * EXPERTS_PER_GROUP + loc0
    e1 = g_idx * EXPERTS_PER_GROUP + loc1
    e_ref[...] = jnp.concatenate([e0, e1, jnp.zeros((SUBLANES - TOP_K, tm), jnp.int32)], axis=0)
    gates_t = jnp.concatenate([p0 / psum, p1 / psum, jnp.zeros((LANES - TOP_K, tm), jnp.float32)], axis=0)
    gc_ref[...] = gates_t.T


def _outproj_call(x, rg_p, ml_p, rg_s, ml_s, wo, g, b, rwt, rb):
    n, d = x.shape
    tm = TM_OUT
    npt = rg_p.shape[0] // tm
    row = pl.BlockSpec((tm, d), lambda i: (i, 0))
    half_p = pl.BlockSpec((tm, D_RG), lambda i: (jnp.minimum(i, npt - 1), 0))
    half_s = pl.BlockSpec((tm, D_RG), lambda i: (jnp.maximum(i - npt, 0), 0))
    vec = pl.BlockSpec((1, d), lambda i: (0, 0))
    return pl.pallas_call(
        functools.partial(_outproj_kernel, n_prompt_tiles=npt),
        grid=(n // tm,),
        in_specs=[row, half_p, half_p, half_s, half_s,
                  pl.BlockSpec((d, d), lambda i: (0, 0), pipeline_mode=pl.Buffered(1)),
                  vec, vec,
                  pl.BlockSpec((N_EXPERTS, d), lambda i: (0, 0)),
                  pl.BlockSpec((N_EXPERTS, 1), lambda i: (0, 0))],
        out_specs=[row, row,
                   pl.BlockSpec((SUBLANES, tm), lambda i: (0, i)),
                   pl.BlockSpec((tm, LANES), lambda i: (i, 0))],
        out_shape=[jax.ShapeDtypeStruct((n, d), jnp.float32),
                   jax.ShapeDtypeStruct((n, d), MXU_DTYPE),
                   jax.ShapeDtypeStruct((SUBLANES, n), jnp.int32),
                   jax.ShapeDtypeStruct((n, LANES), jnp.float32)],
        compiler_params=_cparams(("parallel",)),
        name="outproj_ln_router",
    )(x, rg_p, ml_p, rg_s, ml_s, wo, g.reshape(1, d), b.reshape(1, d), rwt, rb.reshape(N_EXPERTS, 1))


def _dispatch_plan(e_t, tb, nb):
    n = e_t.shape[1]
    p = TOP_K * n
    i32 = jnp.int32
    flat_e = e_t.T.reshape(p)
    sk = jnp.sort(flat_e * p + jnp.arange(p, dtype=i32))
    order = sk % p
    counts = jnp.sum((flat_e[:, None] == jnp.arange(N_EXPERTS, dtype=i32)[None, :]).astype(i32), axis=0)
    start = jnp.cumsum(counts) - counts
    pcounts = (counts + tb - 1) // tb * tb
    pend = jnp.cumsum(pcounts)
    pstart = pend - pcounts
    r = jnp.arange(nb * tb, dtype=i32)
    er = jnp.minimum(jnp.sum((r[:, None] >= pend[None, :]).astype(i32), axis=1), N_EXPERTS - 1)
    off = r - pstart[er]
    real = (r < pend[-1]) & (off < counts[er])
    n_real_before = jnp.where(r < pend[-1], start[er] + jnp.minimum(off, counts[er]), p)
    pair = order[jnp.clip(start[er] + off, 0, p - 1)]
    tok, k = pair // TOP_K, pair % TOP_K
    src_tok = jnp.where(real, tok, 0).astype(i32)
    dst_row = jnp.where(real, k * n + tok, p + (r - n_real_before)).astype(i32)
    blk_e = er[::tb]
    return src_tok, dst_row, blk_e


def _moe_kernel(be_ref, src_ref, srcn_ref, dst_ref, dstp_ref, x_hbm, w1_ref, w3_ref, w2_ref, y_hbm,
                xbuf, ybuf, gsem, ssem):
    r = pl.program_id(0)
    n_r = pl.num_programs(0)
    tb = ybuf.shape[1]
    slot = r % 2
    other = 1 - slot

    def gather_row(idx_ref, s, j):
        pltpu.make_async_copy(x_hbm.at[pl.ds(idx_ref[j], 1), :],
                              xbuf.at[s, pl.ds(j, 1), :], gsem.at[s]).start()

    def scatter_row(idx_ref, s, j):
        pltpu.make_async_copy(ybuf.at[s, pl.ds(j, 1), :],
                              y_hbm.at[pl.ds(idx_ref[j], 1), :], ssem.at[s]).start()

    def wait_gather(s):
        pltpu.make_async_copy(x_hbm.at[pl.ds(0, tb), :], xbuf.at[s], gsem.at[s]).wait()

    def wait_scatter(s):
        pltpu.make_async_copy(ybuf.at[s], y_hbm.at[pl.ds(0, tb), :], ssem.at[s]).wait()

    def expert(s):
        xb = xbuf[s].astype(MXU_DTYPE)
        h1 = jnp.dot(xb, w1_ref[...], preferred_element_type=jnp.float32)
        h3 = jnp.dot(xb, w3_ref[...], preferred_element_type=jnp.float32)
        hb = (jax.nn.silu(h1) * h3).astype(MXU_DTYPE)
        ybuf[s] = jnp.dot(hb, w2_ref[...], preferred_element_type=jnp.float32)

    @pl.when(r == 0)
    def _():
        for j in range(tb):
            gather_row(src_ref, 0, j)
        for j in range(tb):
            gather_row(srcn_ref, 1, j)
        wait_gather(0)
        expert(0)

    @pl.when(r > 0)
    def _():
        for j in range(tb):
            gather_row(srcn_ref, other, j)
            scatter_row(dstp_ref, other, j)
        wait_gather(slot)
        expert(slot)
        wait_scatter(other)

    @pl.when(r == n_r - 1)
    def _():
        wait_gather(other)

        def put(j, carry):
            scatter_row(dst_ref, slot, j)
            return carry
        lax.fori_loop(0, tb, put, 0)
        wait_scatter(slot)


def _moe_call(x1, src_tok, dst_row, blk_e, w1, w3, w2, layer, tb, nb):
    n, d = x1.shape
    f = w1.shape[-1]
    assert nb >= 2

    def wblk(r, be):
        return (layer, be[r], 0, 0)

    grid_spec = pltpu.PrefetchScalarGridSpec(
        num_scalar_prefetch=1,
        grid=(nb,),
        in_specs=[
            pl.BlockSpec((tb,), lambda r, be: (r,), memory_space=pltpu.SMEM),
            pl.BlockSpec((tb,), lambda r, be: (jnp.minimum(r + 1, nb - 1),), memory_space=pltpu.SMEM),
            pl.BlockSpec((tb,), lambda r, be: (r,), memory_space=pltpu.SMEM),
            pl.BlockSpec((tb,), lambda r, be: (jnp.maximum(r - 1, 0),), memory_space=pltpu.SMEM),
            pl.BlockSpec(memory_space=pl.ANY),
            pl.BlockSpec((None, None, d, f), wblk),
            pl.BlockSpec((None, None, d, f), wblk),
            pl.BlockSpec((None, None, f, d), wblk),
        ],
        out_specs=pl.BlockSpec(memory_space=pl.ANY),
        scratch_shapes=[pltpu.VMEM((2, tb, d), jnp.float32), pltpu.VMEM((2, tb, d), jnp.float32),
                        pltpu.SemaphoreType.DMA((2,)), pltpu.SemaphoreType.DMA((2,))],
    )
    return pl.pallas_call(
        _moe_kernel,
        grid_spec=grid_spec,
        out_shape=jax.ShapeDtypeStruct((nb * tb, d), jnp.float32),
        compiler_params=_cparams(("arbitrary",)),
        name="moe_experts",
    )(blk_e, src_tok, src_tok, dst_row, dst_row, x1, w1, w3, w2)


def _combine_math(x_ref, y0_ref, y1_ref, gc_ref, g_ref, b_ref):
    gc = gc_ref[...]
    f = y0_ref[...] * gc[:, 0:1] + y1_ref[...] * gc[:, 1:2]
    return _layer_norm(ALPHA * x_ref[...] + f, g_ref[...], b_ref[...])


def _combine_kernel(x_ref, y0_ref, y1_ref, gc_ref, g_ref, b_ref, o_ref, ob_ref):
    y = _combine_math(x_ref, y0_ref, y1_ref, gc_ref, g_ref, b_ref)
    o_ref[...] = y
    ob_ref[...] = y.astype(ob_ref.dtype)


def _combine_last_kernel(x_ref, y0_ref, y1_ref, gc_ref, g_ref, b_ref, op_ref, os_ref, *, n_prompt_tiles):
    y = _combine_math(x_ref, y0_ref, y1_ref, gc_ref, g_ref, b_ref)
    is_prompt = pl.program_id(0) < n_prompt_tiles

    @pl.when(is_prompt)
    def _():
        op_ref[...] = y

    @pl.when(jnp.logical_not(is_prompt))
    def _():
        os_ref[...] = y


def _combine_call(x1, gcol, ys, g, b, n_p=None):
    n, d = x1.shape
    tm = TM_CMB
    nt = n // tm
    row = pl.BlockSpec((tm, d), lambda i: (i, 0))
    vec = pl.BlockSpec((1, d), lambda i: (0, 0))
    in_specs = [row, row, pl.BlockSpec((tm, d), lambda i: (nt + i, 0)),
                pl.BlockSpec((tm, LANES), lambda i: (i, 0)), vec, vec]
    args = (x1, ys, ys, gcol, g.reshape(1, d), b.reshape(1, d))
    if n_p is None:
        return pl.pallas_call(
            _combine_kernel,
            grid=(nt,),
            in_specs=in_specs,
            out_specs=[row, row],
            out_shape=[jax.ShapeDtypeStruct((n, d), jnp.float32),
                       jax.ShapeDtypeStruct((n, d), MXU_DTYPE)],
            compiler_params=_cparams(("parallel",)),
            name="combine_ln",
        )(*args)
    npt = n_p // tm
    return pl.pallas_call(
        functools.partial(_combine_last_kernel, n_prompt_tiles=npt),
        grid=(nt,),
        in_specs=in_specs,
        out_specs=[pl.BlockSpec((tm, d), lambda i: (jnp.minimum(i, npt - 1), 0)),
                   pl.BlockSpec((tm, d), lambda i: (jnp.maximum(i - npt, 0), 0))],
        out_shape=[jax.ShapeDtypeStruct((n_p, d), jnp.float32),
                   jax.ShapeDtypeStruct((n - n_p, d), jnp.float32)],
        compiler_params=_cparams(("arbitrary",)),
        name="combine_ln_last",
    )(*args)


def kernel(x_prompt, x_sample, state_conv, state_lru, state_mlstm_C, state_mlstm_n, state_mlstm_m,
           ln_in_g, ln_in_b, w_in, b_in, conv_w, conv_b, lru_wa, lru_ba, lru_wx, lru_bx, lru_lambda,
           mlstm_norm_g, w_out, ln1_g, ln1_b, router_w, router_b, w1, w3, w2, ln2_g, ln2_b):
    bp, tp, d = x_prompt.shape
    bs, ts, _ = x_sample.shape
    depth = w_in.shape[0]
    n_p = bp * tp
    n_s = bs * SLOT
    n_pad = n_p + n_s
    assert d == D_MODEL and ts <= min(SLOT, CHUNK) and ts % SUBLANES == 0
    assert tp % T_RG == 0 and tp % L_ML == 0
    assert n_p % TM_PROJ == 0 and n_s % TM_PROJ == 0
    f32 = jnp.float32

    nb = -(-(TOP_K * n_pad + N_EXPERTS * (TB_MOE - 1)) // TB_MOE)

    w_in_m = w_in[:, :, :GATE_OFF].astype(MXU_DTYPE)
    w_gate_m = jnp.pad(w_in[:, :, GATE_OFF:], ((0, 0), (0, 0), (0, LANES - 2 * ML_HEADS))).astype(MXU_DTYPE)
    b_gate = jnp.pad(b_in[:, GATE_OFF:], ((0, 0), (0, LANES - 2 * ML_HEADS)))
    w_rg_m = jnp.concatenate([lru_wa, lru_wx], axis=-1).astype(MXU_DTYPE)
    w_out_m = w_out.astype(MXU_DTYPE)
    rwt_m = router_w.T.astype(MXU_DTYPE)
    w1_m, w3_m, w2_m = w1.astype(MXU_DTYPE), w3.astype(MXU_DTYPE), w2.astype(MXU_DTYPE)

    zp = lambda *s: jnp.zeros(s, f32)
    x, xb = _ln_call(x_prompt.reshape(n_p, d),
                     jnp.pad(x_sample, ((0, 0), (0, SLOT - ts), (0, 0))).reshape(n_s, d), ln_in_g, ln_in_b)
    outs = {k: [] for k in ("pc", "pl", "pC", "pn", "pm", "sc", "sl", "sC", "sn", "sm")}
    for l in range(depth):
        z3 = _in_proj_call(xb, w_in_m, b_in, l)
        gates = _gate_proj_call(xb, w_gate_m[l], b_gate[l].reshape(1, LANES))

        rg_w = (conv_w[l], conv_b[l], w_rg_m[l], lru_ba[l], lru_bx[l], lru_lambda[l])
        rg_p, pc, plru = _rglru_call(z3, zp(bp, CONV_W - 1, D_RG), zp(bp, D_RG), *rg_w, row0=0, seq_stride=tp,
                                     tt=T_RG, tv=T_RG, n_t=tp // T_RG, name="rglru_prompt")
        rg_s, sc, slru = _rglru_call(z3, state_conv[l], state_lru[l], *rg_w, row0=n_p, seq_stride=SLOT,
                                     tt=SLOT, tv=ts, n_t=1, name="rglru_sample")
        ml_p, pC, pn, pm = _mlstm_call(z3, gates, zp(bp, ML_HEADS, ML_DH, ML_DH), zp(bp, ML_HEADS, ML_DH),
                                       zp(bp, ML_HEADS), mlstm_norm_g[l], row0=0, seq_stride=tp,
                                       ll=L_ML, lc_last=min(CHUNK, tp), tv=L_ML, n_c=tp // L_ML, name="mlstm_prompt")
        ml_s, sC, sn, sm = _mlstm_call(z3, gates, state_mlstm_C[l], state_mlstm_n[l], state_mlstm_m[l],
                                       mlstm_norm_g[l], row0=n_p, seq_stride=SLOT,
                                       ll=SLOT, lc_last=SLOT, tv=ts, n_c=1, name="mlstm_sample")

        x1, x1b, e_t, gcol = _outproj_call(x, rg_p, ml_p, rg_s, ml_s, w_out_m[l], ln1_g[l], ln1_b[l],
                                           rwt_m, router_b)
        src_tok, dst_row, blk_e = _dispatch_plan(e_t[:TOP_K], TB_MOE, nb)
        ys = _moe_call(x1, src_tok, dst_row, blk_e, w1_m, w3_m, w2_m, l, TB_MOE, nb)
        if l + 1 < depth:
            x, xb = _combine_call(x1, gcol, ys, ln2_g[l], ln2_b[l])
        else:
            y_p, y_s = _combine_call(x1, gcol, ys, ln2_g[l], ln2_b[l], n_p=n_p)

        for k, v in zip(("pc", "pl", "pC", "pn", "pm", "sc", "sl", "sC", "sn", "sm"),
                        (pc, plru, pC, pn, pm, sc, slru, sC, sn, sm)):
            outs[k].append(v)

    y_prompt = y_p.reshape(bp, tp, d)
    y_sample = y_s.reshape(bs, SLOT, d)[:, :ts]
    st = {k: jnp.stack(v) for k, v in outs.items()}
    return (y_prompt, y_sample, st["pc"], st["pl"], st["pC"], st["pn"], st["pm"],
            st["sc"], st["sl"], st["sC"], st["sn"], st["sm"])
```

```python
import functools

import jax
import jax.numpy as jnp
from jax import lax
from jax.experimental import pallas as pl
from jax.experimental.pallas import tpu as pltpu

DEPTH = 4
D_MODEL = 2048
D_RG = 1024
RG_BLOCKS = 8
RG_BW = D_RG // RG_BLOCKS
CONV_W = 4
LRU_C = 8.0
D_ML = 1024
ML_HEADS = 4
ML_DH = D_ML // ML_HEADS
CHUNK = 64
N_EXPERTS = 16
N_GROUPS = 4
EXPERTS_PER_GROUP = N_EXPERTS // N_GROUPS
TOP_K = 2
D_FF = D_MODEL // 2
ALPHA = (2 * DEPTH) ** 0.25
LN_EPS = 1e-5
RMS_EPS = 1e-6
N_PROJ = 6
GATE_OFF = N_PROJ * D_RG

LANES = 128
SUBLANES = 8
MXU_DTYPE = jnp.bfloat16
SLOT = 128
TM_PROJ = 1024
TM_LN = 512
TM_OUT = 512
TM_CMB = 512
TB_MOE = 256
T_RG = 512
L_ML = 256
VMEM_LIMIT = 56 * 1024 * 1024


def _cparams(sem):
    return pltpu.CompilerParams(dimension_semantics=sem, vmem_limit_bytes=VMEM_LIMIT)


def _layer_norm(h, g, b):
    mu = jnp.mean(h, axis=-1, keepdims=True)
    hc = h - mu
    var = jnp.mean(hc * hc, axis=-1, keepdims=True)
    return hc * lax.rsqrt(var + LN_EPS) * g + b


def _ln_kernel(xp_ref, xs_ref, g_ref, b_ref, o_ref, ob_ref, *, n_prompt_tiles):
    x = jnp.where(pl.program_id(0) < n_prompt_tiles, xp_ref[...], xs_ref[...])
    y = _layer_norm(x, g_ref[...], b_ref[...])
    o_ref[...] = y
    ob_ref[...] = y.astype(ob_ref.dtype)


def _ln_call(xp, xs, g, b):
    (n_p, d), n_s = xp.shape, xs.shape[0]
    tm = TM_LN
    npt = n_p // tm
    n = n_p + n_s
    row = pl.BlockSpec((tm, d), lambda i: (i, 0))
    vec = pl.BlockSpec((1, d), lambda i: (0, 0))
    return pl.pallas_call(
        functools.partial(_ln_kernel, n_prompt_tiles=npt),
        grid=(n // tm,),
        in_specs=[pl.BlockSpec((tm, d), lambda i: (jnp.minimum(i, npt - 1), 0)),
                  pl.BlockSpec((tm, d), lambda i: (jnp.maximum(i - npt, 0), 0)),
                  vec, vec],
        out_specs=[row, row],
        out_shape=[jax.ShapeDtypeStruct((n, d), jnp.float32),
                   jax.ShapeDtypeStruct((n, d), MXU_DTYPE)],
        compiler_params=_cparams(("parallel",)),
        name="ln_in",
    )(xp, xs, g.reshape(1, d), b.reshape(1, d))


def _proj_kernel(x_ref, w_ref, b_ref, o_ref):
    o_ref[...] = jnp.dot(x_ref[...], w_ref[...], preferred_element_type=jnp.float32) + b_ref[...]


def _in_proj_call(xb, w_in_m, b_in, layer):
    n, d = xb.shape
    depth, d_in = b_in.shape
    return pl.pallas_call(
        _proj_kernel,
        grid=(N_PROJ, n // TM_PROJ),
        in_specs=[pl.BlockSpec((TM_PROJ, d), lambda j, i: (i, 0)),
                  pl.BlockSpec((None, d, D_RG), lambda j, i: (layer, 0, j)),
                  pl.BlockSpec((None, 1, D_RG), lambda j, i: (layer, 0, j))],
        out_specs=pl.BlockSpec((None, TM_PROJ, D_RG), lambda j, i: (j, i, 0)),
        out_shape=jax.ShapeDtypeStruct((N_PROJ, n, D_RG), jnp.float32),
        compiler_params=_cparams(("parallel", "parallel")),
        name="in_proj",
    )(xb, w_in_m, b_in.reshape(depth, 1, d_in))


def _gate_proj_call(xb, wg, bg):
    n, d = xb.shape
    return pl.pallas_call(
        _proj_kernel,
        grid=(n // TM_PROJ,),
        in_specs=[pl.BlockSpec((TM_PROJ, d), lambda i: (i, 0)),
                  pl.BlockSpec((d, LANES), lambda i: (0, 0)),
                  pl.BlockSpec((1, LANES), lambda i: (0, 0))],
        out_specs=pl.BlockSpec((TM_PROJ, LANES), lambda i: (i, 0)),
        out_shape=jax.ShapeDtypeStruct((n, LANES), jnp.float32),
        compiler_params=_cparams(("parallel",)),
        name="gate_proj",
    )(xb, wg, bg)


def _rglru_kernel(xr_ref, gr_ref, cs_ref, h0_ref, cw_ref, cb_ref, wg_ref, ba_ref, bx_ref, lam_ref,
                  mix_ref, cso_ref, ho_ref, ext_ref, a_ref, u_ref, hc_ref, *, tv, n_t):
    t = pl.program_id(1)
    tt = mix_ref.shape[0]

    @pl.when(t == 0)
    def _():
        ext_ref[0:SUBLANES, :] = jnp.zeros((SUBLANES, D_RG), jnp.float32)
        ext_ref[SUBLANES - (CONV_W - 1):SUBLANES, :] = cs_ref[0]
        hc_ref[...] = h0_ref[0]

    u_in = xr_ref[0:tv, :]
    ext_ref[SUBLANES:SUBLANES + tv, :] = u_in
    cw = cw_ref[...]
    xc = cb_ref[...]
    for tap in range(CONV_W - 1):
        off = SUBLANES - (CONV_W - 1) + tap
        xc = xc + ext_ref[off:off + tv, :] * cw[tap:tap + 1, :]
    xc = xc + u_in * cw[CONV_W - 1:CONV_W, :]

    @pl.when(t == n_t - 1)
    def _():
        cso_ref[0] = ext_ref[SUBLANES + tv - (CONV_W - 1):SUBLANES + tv, :]

    ext_ref[0:SUBLANES, :] = ext_ref[tv:tv + SUBLANES, :]

    xcb = xc.astype(MXU_DTYPE)
    rowmod = lax.broadcasted_iota(jnp.int32, (tv, RG_BW), 0) % SUBLANES
    for nb in range(RG_BLOCKS):
        cs = slice(nb * RG_BW, (nb + 1) * RG_BW)
        pre = jnp.dot(xcb[:, cs], wg_ref[nb], preferred_element_type=jnp.float32)
        r = jax.nn.sigmoid(pre[:, :RG_BW] + ba_ref[:, cs])
        ig = jax.nn.sigmoid(pre[:, RG_BW:] + bx_ref[:, cs])
        log_a = LRU_C * r * jax.nn.log_sigmoid(lam_ref[:, cs])
        a = jnp.exp(log_a)
        th = jnp.tanh(log_a)
        u = jnp.sqrt(-2.0 * th / (1.0 - th)) * (ig * xc[:, cs])
        for s in (1, 2, 4):
            a_s = pltpu.roll(a, s, 0)
            u_s = pltpu.roll(u, s, 0)
            m = rowmod >= s
            u = jnp.where(m, a * u_s + u, u)
            a = jnp.where(m, a * a_s, a)
        a_ref[0:tv, cs] = a
        u_ref[0:tv, cs] = u

    def body(g, carry):
        rows = pl.ds(pl.multiple_of(g * SUBLANES, SUBLANES), SUBLANES)
        h = a_ref[rows, :] * carry + u_ref[rows, :]
        u_ref[rows, :] = h
        return h[SUBLANES - 1:SUBLANES, :]

    h_last = lax.fori_loop(0, tv // SUBLANES, body, hc_ref[...])
    hc_ref[...] = h_last
    ho_ref[0] = h_last

    mix_ref[0:tv, :] = (jax.nn.gelu(gr_ref[0:tv, :]) * u_ref[0:tv, :]).astype(mix_ref.dtype)
    if tv < tt:
        mix_ref[tv:tt, :] = jnp.zeros((tt - tv, D_RG), mix_ref.dtype)


def _rglru_call(z3, conv0, h0, cw, cb, wg, ba, bx, lam, *, row0, seq_stride, tt, tv, n_t, name):
    n_seq = conv0.shape[0]
    blk0, sb = row0 // tt, seq_stride // tt

    def rows(b, t):
        return blk0 + b * sb + t

    vec = pl.BlockSpec((1, D_RG), lambda b, t: (0, 0))
    in_specs = [
        pl.BlockSpec((None, tt, D_RG), lambda b, t: (0, rows(b, t), 0)),
        pl.BlockSpec((None, tt, D_RG), lambda b, t: (1, rows(b, t), 0)),
        pl.BlockSpec((1, CONV_W - 1, D_RG), lambda b, t: (b, 0, 0)),
        pl.BlockSpec((1, 1, D_RG), lambda b, t: (b, 0, 0)),
        pl.BlockSpec((CONV_W, D_RG), lambda b, t: (0, 0)),
        vec,
        pl.BlockSpec((RG_BLOCKS, RG_BW, 2 * RG_BW), lambda b, t: (0, 0, 0)),
        vec, vec, vec,
    ]
    args = [z3, z3, conv0, h0.reshape(n_seq, 1, D_RG), cw, cb.reshape(1, D_RG), wg,
            ba.reshape(1, D_RG), bx.reshape(1, D_RG), lam.reshape(1, D_RG)]
    mix, conv_out, h_out = pl.pallas_call(
        functools.partial(_rglru_kernel, tv=tv, n_t=n_t),
        grid=(n_seq, n_t),
        in_specs=in_specs,
        out_specs=[pl.BlockSpec((tt, D_RG), lambda b, t: (b * sb + t, 0)),
                   pl.BlockSpec((1, CONV_W - 1, D_RG), lambda b, t: (b, 0, 0)),
                   pl.BlockSpec((1, 1, D_RG), lambda b, t: (b, 0, 0))],
        out_shape=[jax.ShapeDtypeStruct((n_seq * seq_stride, D_RG), MXU_DTYPE),
                   jax.ShapeDtypeStruct((n_seq, CONV_W - 1, D_RG), jnp.float32),
                   jax.ShapeDtypeStruct((n_seq, 1, D_RG), jnp.float32)],
        scratch_shapes=[pltpu.VMEM((tt + SUBLANES, D_RG), jnp.float32),
                        pltpu.VMEM((tt, D_RG), jnp.float32),
                        pltpu.VMEM((tt, D_RG), jnp.float32),
                        pltpu.VMEM((1, D_RG), jnp.float32)],
        compiler_params=_cparams(("parallel", "arbitrary")),
        name=name,
    )(*args)
    return mix, conv_out, h_out.reshape(n_seq, D_RG)


def _mlstm_kernel(q_ref, k_ref, v_ref, o_ref, g_ref, c0_ref, n0_ref, m0_ref, ng_ref,
                  mix_ref, co_ref, no_ref, mo_ref, c_s, n_s, m_s, *, lc_last, tv, n_c):
    c = pl.program_id(1)
    ll = mix_ref.shape[0]
    h_n = ML_HEADS

    @pl.when(c == 0)
    def _():
        c_s[...] = c0_ref[0]
        n_s[...] = n0_ref[0]
        m_s[...] = m0_ref[0]

    dn_t = (((1,), (1,)), ((), ()))
    dn_c0 = (((0,), (0,)), ((), ()))

    def run(lc):
        g = g_ref[...]
        row = lax.broadcasted_iota(jnp.int32, (ll, LANES), 0)
        lf = jax.nn.log_sigmoid(g)
        if tv < ll:
            lane = lax.broadcasted_iota(jnp.int32, (ll, LANES), 1)
            g = jnp.where((row >= tv) & (lane < h_n), -jnp.inf, g)
            lf = jnp.where(row >= tv, 0.0, lf)
        rowc = row % lc
        bcum = lf
        s = 1
        while s < lc:
            bcum = bcum + jnp.where(rowc >= s, pltpu.roll(bcum, s, 0), 0.0)
            s *= 2
        g_t = g.T
        b_t = bcum.T
        ti = lax.broadcasted_iota(jnp.int32, (lc, lc), 0)
        si = lax.broadcasted_iota(jnp.int32, (lc, lc), 1)
        causal = si <= ti

        m_val = [m_s[:, h:h + 1] for h in range(h_n)]
        n_val = [n_s[h:h + 1, :] for h in range(h_n)]
        for j in range(ll // lc):
            rs = slice(j * lc, (j + 1) * lc)
            for h in range(h_n):
                cs = slice(h * ML_DH, (h + 1) * ML_DH)
                qf = q_ref[rs, cs]
                kf = k_ref[rs, cs] * (ML_DH ** -0.5)
                vf = v_ref[rs, cs]
                qb, kb, vb = qf.astype(MXU_DTYPE), kf.astype(MXU_DTYPE), vf.astype(MXU_DTYPE)
                ig_col = g[rs, h:h + 1]
                b_col = bcum[rs, h_n + h:h_n + h + 1]
                ig_row = g_t[h:h + 1, rs]
                b_row = b_t[h_n + h:h_n + h + 1, rs]
                m_prev = m_val[h]

                logw = jnp.where(causal, b_col - b_row + ig_row, -jnp.inf)
                gq = b_col + m_prev
                mt = jnp.maximum(gq, jnp.max(logw, axis=-1, keepdims=True))
                w_intra = jnp.exp(logw - mt)
                w_state = jnp.exp(gq - mt)
                s_mat = lax.dot_general(qb, kb, dn_t, preferred_element_type=jnp.float32) * w_intra
                c_prev = c_s[h]
                n_prev = n_val[h]
                num = (w_state * jnp.dot(qb, c_prev.astype(MXU_DTYPE), preferred_element_type=jnp.float32)
                       + jnp.dot(s_mat.astype(MXU_DTYPE), vb, preferred_element_type=jnp.float32))
                den = (w_state * jnp.sum(qf * n_prev, axis=-1, keepdims=True)
                       + jnp.sum(s_mat, axis=-1, keepdims=True))
                hh = num / jnp.maximum(jnp.abs(den), jnp.exp(-mt))

                b_last = b_col[lc - 1:lc, :]
                logu_row = b_last - b_row + ig_row
                m_new = jnp.maximum(b_last + m_prev, jnp.max(logu_row, axis=-1, keepdims=True))
                u_col = jnp.exp(b_last - b_col + ig_col - m_new)
                decay = jnp.exp(b_last + m_prev - m_new)
                ku = kf * u_col
                c_s[h] = decay * c_prev + lax.dot_general(ku.astype(MXU_DTYPE), vb, dn_c0,
                                                          preferred_element_type=jnp.float32)
                n_val[h] = decay * n_prev + jnp.sum(ku, axis=0, keepdims=True)
                m_val[h] = m_new

                hm = hh * lax.rsqrt(jnp.mean(hh * hh, axis=-1, keepdims=True) + RMS_EPS) * ng_ref[:, cs]
                mix_ref[rs, cs] = (hm * jax.nn.sigmoid(o_ref[rs, cs])).astype(mix_ref.dtype)

        for h in range(h_n):
            m_s[:, h:h + 1] = m_val[h]
            n_s[h:h + 1, :] = n_val[h]

    if lc_last == ll:
        run(ll)
    else:
        @pl.when(c < n_c - 1)
        def _():
            run(ll)

        @pl.when(c == n_c - 1)
        def _():
            run(lc_last)

    @pl.when(c == n_c - 1)
    def _():
        co_ref[0] = c_s[...]
        no_ref[0] = n_s[...]
        mo_ref[0] = m_s[...]


def _mlstm_call(z3, gates, c0, n0, m0, ng, *, row0, seq_stride, ll, lc_last, tv, n_c, name):
    n_seq = c0.shape[0]
    blk0, sb = row0 // ll, seq_stride // ll

    def rows(b, c):
        return blk0 + b * sb + c

    def zspec(j):
        return pl.BlockSpec((None, ll, D_ML), lambda b, c: (j, rows(b, c), 0))

    in_specs = [
        zspec(2), zspec(3), zspec(4), zspec(5),
        pl.BlockSpec((ll, LANES), lambda b, c: (rows(b, c), 0)),
        pl.BlockSpec((1, ML_HEADS, ML_DH, ML_DH), lambda b, c: (b, 0, 0, 0)),
        pl.BlockSpec((1, ML_HEADS, ML_DH), lambda b, c: (b, 0, 0)),
        pl.BlockSpec((1, 1, ML_HEADS), lambda b, c: (b, 0, 0)),
        pl.BlockSpec((1, D_ML), lambda b, c: (0, 0)),
    ]
    args = [z3, z3, z3, z3, gates, c0, n0, m0.reshape(n_seq, 1, ML_HEADS), ng.reshape(1, D_ML)]
    mix, c_out, n_out, m_out = pl.pallas_call(
        functools.partial(_mlstm_kernel, lc_last=lc_last, tv=tv, n_c=n_c),
        grid=(n_seq, n_c),
        in_specs=in_specs,
        out_specs=[pl.BlockSpec((ll, D_ML), lambda b, c: (b * sb + c, 0)),
                   pl.BlockSpec((1, ML_HEADS, ML_DH, ML_DH), lambda b, c: (b, 0, 0, 0)),
                   pl.BlockSpec((1, ML_HEADS, ML_DH), lambda b, c: (b, 0, 0)),
                   pl.BlockSpec((1, 1, ML_HEADS), lambda b, c: (b, 0, 0))],
        out_shape=[jax.ShapeDtypeStruct((n_seq * seq_stride, D_ML), MXU_DTYPE),
                   jax.ShapeDtypeStruct((n_seq, ML_HEADS, ML_DH, ML_DH), jnp.float32),
                   jax.ShapeDtypeStruct((n_seq, ML_HEADS, ML_DH), jnp.float32),
                   jax.ShapeDtypeStruct((n_seq, 1, ML_HEADS), jnp.float32)],
        scratch_shapes=[pltpu.VMEM((ML_HEADS, ML_DH, ML_DH), jnp.float32),
                        pltpu.VMEM((ML_HEADS, ML_DH), jnp.float32),
                        pltpu.VMEM((1, ML_HEADS), jnp.float32)],
        compiler_params=_cparams(("parallel", "arbitrary")),
        name=name,
    )(*args)
    return mix, c_out, n_out, m_out.reshape(n_seq, ML_HEADS)


def _first_argmax(vals):
    best, idx = vals[0], jnp.zeros(vals[0].shape, jnp.int32)
    for j in range(1, len(vals)):
        better = vals[j] > best
        best = jnp.where(better, vals[j], best)
        idx = jnp.where(better, j, idx)
    return best, idx


def _pick(vals, idx):
    out = vals[0]
    for j in range(1, len(vals)):
        out = jnp.where(idx == j, vals[j], out)
    return out


def _outproj_kernel(x_ref, rgp_ref, mlp_ref, rgs_ref, mls_ref, wo_ref, g_ref, b_ref, rwt_ref, rb_ref,
                    x1_ref, x1b_ref, e_ref, gc_ref, *, n_prompt_tiles):
    tm = x_ref.shape[0]
    is_prompt = pl.program_id(0) < n_prompt_tiles
    mix_rg = jnp.where(is_prompt, rgp_ref[...], rgs_ref[...])
    mix_ml = jnp.where(is_prompt, mlp_ref[...], mls_ref[...])
    y = (jnp.dot(mix_rg, wo_ref[0:D_RG, :], preferred_element_type=jnp.float32)
         + jnp.dot(mix_ml, wo_ref[D_RG:, :], preferred_element_type=jnp.float32))
    x1 = _layer_norm(ALPHA * x_ref[...] + y, g_ref[...], b_ref[...])
    x1_ref[...] = x1
    x1b = x1.astype(x1b_ref.dtype)
    x1b_ref[...] = x1b

    logits = lax.dot_general(rwt_ref[...], x1b, (((1,), (1,)), ((), ())),
                             preferred_element_type=jnp.float32)
    ex = jnp.exp(logits - jnp.max(logits, axis=0, keepdims=True))
    probs = ex / jnp.sum(ex, axis=0, keepdims=True)
    sel = probs + rb_ref[...]
    sel_r = [sel[e:e + 1, :] for e in range(N_EXPERTS)]
    prob_r = [probs[e:e + 1, :] for e in range(N_EXPERTS)]
    scores = []
    for gi in range(N_GROUPS):
        v = sel_r[gi * EXPERTS_PER_GROUP:(gi + 1) * EXPERTS_PER_GROUP]
        top2 = None
        for i in range(EXPERTS_PER_GROUP):
            for j in range(i + 1, EXPERTS_PER_GROUP):
                pair = v[i] + v[j]
                top2 = pair if top2 is None else jnp.maximum(top2, pair)
        scores.append(top2)
    _, g_idx = _first_argmax(scores)
    in_sel = [_pick([sel_r[gi * EXPERTS_PER_GROUP + j] for gi in range(N_GROUPS)], g_idx)
              for j in range(EXPERTS_PER_GROUP)]
    in_prob = [_pick([prob_r[gi * EXPERTS_PER_GROUP + j] for gi in range(N_GROUPS)], g_idx)
               for j in range(EXPERTS_PER_GROUP)]
    _, loc0 = _first_argmax(in_sel)
    masked = [jnp.where(loc0 == j, -jnp.inf, in_sel[j]) for j in range(EXPERTS_PER_GROUP)]
    _, loc1 = _first_argmax(masked)
    p0, p1 = _pick(in_prob, loc0), _pick(in_prob, loc1)
    psum = p0 + p1
    e0 = g_idx * EXPERTS_PER_GROUP + loc0
    e1 = g_idx * EXPERTS_PER_GROUP + loc1
    e_ref[...] = jnp.concatenate([e0, e1, jnp.zeros((SUBLANES - TOP_K, tm), jnp.int32)], axis=0)
    gates_t = jnp.concatenate([p0 / psum, p1 / psum, jnp.zeros((LANES - TOP_K, tm), jnp.float32)], axis=0)
    gc_ref[...] = gates_t.T


def _outproj_call(x, rg_p, ml_p, rg_s, ml_s, wo, g, b, rwt, rb):
    n, d = x.shape
    tm = TM_OUT
    npt = rg_p.shape[0] // tm
    row = pl.BlockSpec((tm, d), lambda i: (i, 0))
    half_p = pl.BlockSpec((tm, D_RG), lambda i: (jnp.minimum(i, npt - 1), 0))
    half_s = pl.BlockSpec((tm, D_RG), lambda i: (jnp.maximum(i - npt, 0), 0))
    vec = pl.BlockSpec((1, d), lambda i: (0, 0))
    return pl.pallas_call(
        functools.partial(_outproj_kernel, n_prompt_tiles=npt),
        grid=(n // tm,),
        in_specs=[row, half_p, half_p, half_s, half_s,
                  pl.BlockSpec((d, d), lambda i: (0, 0), pipeline_mode=pl.Buffered(1)),
                  vec, vec,
                  pl.BlockSpec((N_EXPERTS, d), lambda i: (0, 0)),
                  pl.BlockSpec((N_EXPERTS, 1), lambda i: (0, 0))],
        out_specs=[row, row,
                   pl.BlockSpec((SUBLANES, tm), lambda i: (0, i)),
                   pl.BlockSpec((tm, LANES), lambda i: (i, 0))],
        out_shape=[jax.ShapeDtypeStruct((n, d), jnp.float32),
                   jax.ShapeDtypeStruct((n, d), MXU_DTYPE),
                   jax.ShapeDtypeStruct((SUBLANES, n), jnp.int32),
                   jax.ShapeDtypeStruct((n, LANES), jnp.float32)],
        compiler_params=_cparams(("parallel",)),
        name="outproj_ln_router",
    )(x, rg_p, ml_p, rg_s, ml_s, wo, g.reshape(1, d), b.reshape(1, d), rwt, rb.reshape(N_EXPERTS, 1))


def _dispatch_plan(e_t, tb, nb):
    n = e_t.shape[1]
    p = TOP_K * n
    i32 = jnp.int32
    flat_e = e_t.T.reshape(p)
    sk = jnp.sort(flat_e * p + jnp.arange(p, dtype=i32))
    order = sk % p
    counts = jnp.sum((flat_e[:, None] == jnp.arange(N_EXPERTS, dtype=i32)[None, :]).astype(i32), axis=0)
    start = jnp.cumsum(counts) - counts
    pcounts = (counts + tb - 1) // tb * tb
    pend = jnp.cumsum(pcounts)
    pstart = pend - pcounts
    r = jnp.arange(nb * tb, dtype=i32)
    er = jnp.minimum(jnp.sum((r[:, None] >= pend[None, :]).astype(i32), axis=1), N_EXPERTS - 1)
    off = r - pstart[er]
    real = (r < pend[-1]) & (off < counts[er])
    n_real_before = jnp.where(r < pend[-1], start[er] + jnp.minimum(off, counts[er]), p)
    pair = order[jnp.clip(start[er] + off, 0, p - 1)]
    tok, k = pair // TOP_K, pair % TOP_K
    src_tok = jnp.where(real, tok, 0).astype(i32)
    dst_row = jnp.where(real, k * n + tok, p + (r - n_real_before)).astype(i32)
    blk_e = er[::tb]
    return src_tok, dst_row, blk_e


def _moe_kernel(be_ref, src_ref, srcn_ref, dst_ref, dstp_ref, x_hbm, w1_ref, w3_ref, w2_ref, y_hbm,
                xbuf, ybuf, gsem, ssem):
    r = pl.program_id(0)
    n_r = pl.num_programs(0)
    tb = ybuf.shape[1]
    slot = r % 2
    other = 1 - slot

    def gather_row(idx_ref, s, j):
        pltpu.make_async_copy(x_hbm.at[pl.ds(idx_ref[j], 1), :],
                              xbuf.at[s, pl.ds(j, 1), :], gsem.at[s]).start()

    def scatter_row(idx_ref, s, j):
        pltpu.make_async_copy(ybuf.at[s, pl.ds(j, 1), :],
                              y_hbm.at[pl.ds(idx_ref[j], 1), :], ssem.at[s]).start()

    def wait_gather(s):
        pltpu.make_async_copy(x_hbm.at[pl.ds(0, tb), :], xbuf.at[s], gsem.at[s]).wait()

    def wait_scatter(s):
        pltpu.make_async_copy(ybuf.at[s], y_hbm.at[pl.ds(0, tb), :], ssem.at[s]).wait()

    def expert(s):
        xb = xbuf[s].astype(MXU_DTYPE)
        h1 = jnp.dot(xb, w1_ref[...], preferred_element_type=jnp.float32)
        h3 = jnp.dot(xb, w3_ref[...], preferred_element_type=jnp.float32)
        hb = (jax.nn.silu(h1) * h3).astype(MXU_DTYPE)
        ybuf[s] = jnp.dot(hb, w2_ref[...], preferred_element_type=jnp.float32)

    @pl.when(r == 0)
    def _():
        for j in range(tb):
            gather_row(src_ref, 0, j)
        for j in range(tb):
            gather_row(srcn_ref, 1, j)
        wait_gather(0)
        expert(0)

    @pl.when(r > 0)
    def _():
        for j in range(tb):
            gather_row(srcn_ref, other, j)
        wait_gather(slot)
        for j in range(tb):
            scatter_row(dstp_ref, other, j)
        expert(slot)
        wait_scatter(other)

    @pl.when(r == n_r - 1)
    def _():
        wait_gather(other)

        def put(j, carry):
            scatter_row(dst_ref, slot, j)
            return carry
        lax.fori_loop(0, tb, put, 0)
        wait_scatter(slot)


def _moe_call(x1, src_tok, dst_row, blk_e, w1, w3, w2, layer, tb, nb):
    n, d = x1.shape
    f = w1.shape[-1]
    assert nb >= 2

    def wblk(r, be):
        return (layer, be[r], 0, 0)

    grid_spec = pltpu.PrefetchScalarGridSpec(
        num_scalar_prefetch=1,
        grid=(nb,),
        in_specs=[
            pl.BlockSpec((tb,), lambda r, be: (r,), memory_space=pltpu.SMEM),
            pl.BlockSpec((tb,), lambda r, be: (jnp.minimum(r + 1, nb - 1),), memory_space=pltpu.SMEM),
            pl.BlockSpec((tb,), lambda r, be: (r,), memory_space=pltpu.SMEM),
            pl.BlockSpec((tb,), lambda r, be: (jnp.maximum(r - 1, 0),), memory_space=pltpu.SMEM),
            pl.BlockSpec(memory_space=pl.ANY),
            pl.BlockSpec((None, None, d, f), wblk),
            pl.BlockSpec((None, None, d, f), wblk),
            pl.BlockSpec((None, None, f, d), wblk),
        ],
        out_specs=pl.BlockSpec(memory_space=pl.ANY),
        scratch_shapes=[pltpu.VMEM((2, tb, d), jnp.float32), pltpu.VMEM((2, tb, d), jnp.float32),
                        pltpu.SemaphoreType.DMA((2,)), pltpu.SemaphoreType.DMA((2,))],
    )
    return pl.pallas_call(
        _moe_kernel,
        grid_spec=grid_spec,
        out_shape=jax.ShapeDtypeStruct((nb * tb, d), jnp.float32),
        compiler_params=_cparams(("arbitrary",)),
        name="moe_experts",
    )(blk_e, src_tok, src_tok, dst_row, dst_row, x1, w1, w3, w2)


def _combine_math(x_ref, y0_ref, y1_ref, gc_ref, g_ref, b_ref):
    gc = gc_ref[...]
    f = y0_ref[...] * gc[:, 0:1] + y1_ref[...] * gc[:, 1:2]
    return _layer_norm(ALPHA * x_ref[...] + f, g_ref[...], b_ref[...])


def _combine_kernel(x_ref, y0_ref, y1_ref, gc_ref, g_ref, b_ref, o_ref, ob_ref):
    y = _combine_math(x_ref, y0_ref, y1_ref, gc_ref, g_ref, b_ref)
    o_ref[...] = y
    ob_ref[...] = y.astype(ob_ref.dtype)


def _combine_last_kernel(x_ref, y0_ref, y1_ref, gc_ref, g_ref, b_ref, op_ref, os_ref, *, n_prompt_tiles):
    y = _combine_math(x_ref, y0_ref, y1_ref, gc_ref, g_ref, b_ref)
    is_prompt = pl.program_id(0) < n_prompt_tiles

    @pl.when(is_prompt)
    def _():
        op_ref[...] = y

    @pl.when(jnp.logical_not(is_prompt))
    def _():
        os_ref[...] = y


def _combine_call(x1, gcol, ys, g, b, n_p=None):
    n, d = x1.shape
    tm = TM_CMB
    nt = n // tm
    row = pl.BlockSpec((tm, d), lambda i: (i, 0))
    vec = pl.BlockSpec((1, d), lambda i: (0, 0))
    in_specs = [row, row, pl.BlockSpec((tm, d), lambda i: (nt + i, 0)),
                pl.BlockSpec((tm, LANES), lambda i: (i, 0)), vec, vec]
    args = (x1, ys, ys, gcol, g.reshape(1, d), b.reshape(1, d))
    if n_p is None:
        return pl.pallas_call(
            _combine_kernel,
            grid=(nt,),
            in_specs=in_specs,
            out_specs=[row, row],
            out_shape=[jax.ShapeDtypeStruct((n, d), jnp.float32),
                       jax.ShapeDtypeStruct((n, d), MXU_DTYPE)],
            compiler_params=_cparams(("parallel",)),
            name="combine_ln",
        )(*args)
    npt = n_p // tm
    return pl.pallas_call(
        functools.partial(_combine_last_kernel, n_prompt_tiles=npt),
        grid=(nt,),
        in_specs=in_specs,
        out_specs=[pl.BlockSpec((tm, d), lambda i: (jnp.minimum(i, npt - 1), 0)),
                   pl.BlockSpec((tm, d), lambda i: (jnp.maximum(i - npt, 0), 0))],
        out_shape=[jax.ShapeDtypeStruct((n_p, d), jnp.float32),
                   jax.ShapeDtypeStruct((n - n_p, d), jnp.float32)],
        compiler_params=_cparams(("arbitrary",)),
        name="combine_ln_last",
    )(*args)


def kernel(x_prompt, x_sample, state_conv, state_lru, state_mlstm_C, state_mlstm_n, state_mlstm_m,
           ln_in_g, ln_in_b, w_in, b_in, conv_w, conv_b, lru_wa, lru_ba, lru_wx, lru_bx, lru_lambda,
           mlstm_norm_g, w_out, ln1_g, ln1_b, router_w, router_b, w1, w3, w2, ln2_g, ln2_b):
    bp, tp, d = x_prompt.shape
    bs, ts, _ = x_sample.shape
    depth = w_in.shape[0]
    n_p = bp * tp
    n_s = bs * SLOT
    n_pad = n_p + n_s
    assert d == D_MODEL and ts <= min(SLOT, CHUNK) and ts % SUBLANES == 0
    assert tp % T_RG == 0 and tp % L_ML == 0
    assert n_p % TM_PROJ == 0 and n_s % TM_PROJ == 0
    f32 = jnp.float32

    nb = -(-(TOP_K * n_pad + N_EXPERTS * (TB_MOE - 1)) // TB_MOE)

    w_in_m = w_in.astype(MXU_DTYPE)
    w_gate_m = jnp.pad(w_in_m[:, :, GATE_OFF:], ((0, 0), (0, 0), (0, LANES - 2 * ML_HEADS)))
    b_gate = jnp.pad(b_in[:, GATE_OFF:], ((0, 0), (0, LANES - 2 * ML_HEADS)))
    w_rg_m = jnp.concatenate([lru_wa, lru_wx], axis=-1).astype(MXU_DTYPE)
    w_out_m = w_out.astype(MXU_DTYPE)
    rwt_m = router_w.T.astype(MXU_DTYPE)
    w1_m, w3_m, w2_m = w1.astype(MXU_DTYPE), w3.astype(MXU_DTYPE), w2.astype(MXU_DTYPE)

    zp = lambda *s: jnp.zeros(s, f32)
    x, xb = _ln_call(x_prompt.reshape(n_p, d),
                     jnp.pad(x_sample, ((0, 0), (0, SLOT - ts), (0, 0))).reshape(n_s, d), ln_in_g, ln_in_b)
    outs = {k: [] for k in ("pc", "pl", "pC", "pn", "pm", "sc", "sl", "sC", "sn", "sm")}
    for l in range(depth):
        z3 = _in_proj_call(xb, w_in_m, b_in, l)
        gates = _gate_proj_call(xb, w_gate_m[l], b_gate[l].reshape(1, LANES))

        rg_w = (conv_w[l], conv_b[l], w_rg_m[l], lru_ba[l], lru_bx[l], lru_lambda[l])
        rg_p, pc, plru = _rglru_call(z3, zp(bp, CONV_W - 1, D_RG), zp(bp, D_RG), *rg_w, row0=0, seq_stride=tp,
                                     tt=T_RG, tv=T_RG, n_t=tp // T_RG, name="rglru_prompt")
        rg_s, sc, slru = _rglru_call(z3, state_conv[l], state_lru[l], *rg_w, row0=n_p, seq_stride=SLOT,
                                     tt=SLOT, tv=ts, n_t=1, name="rglru_sample")
        ml_p, pC, pn, pm = _mlstm_call(z3, gates, zp(bp, ML_HEADS, ML_DH, ML_DH), zp(bp, ML_HEADS, ML_DH),
                                       zp(bp, ML_HEADS), mlstm_norm_g[l], row0=0, seq_stride=tp,
                                       ll=L_ML, lc_last=min(CHUNK, tp), tv=L_ML, n_c=tp // L_ML, name="mlstm_prompt")
        ml_s, sC, sn, sm = _mlstm_call(z3, gates, state_mlstm_C[l], state_mlstm_n[l], state_mlstm_m[l],
                                       mlstm_norm_g[l], row0=n_p, seq_stride=SLOT,
                                       ll=SLOT, lc_last=SLOT, tv=ts, n_c=1, name="mlstm_sample")

        x1, x1b, e_t, gcol = _outproj_call(x, rg_p, ml_p, rg_s, ml_s, w_out_m[l], ln1_g[l], ln1_b[l],
                                           rwt_m, router_b)
        src_tok, dst_row, blk_e = _dispatch_plan(e_t[:TOP_K], TB_MOE, nb)
        ys = _moe_call(x1, src_tok, dst_row, blk_e, w1_m, w3_m, w2_m, l, TB_MOE, nb)
        if l + 1 < depth:
            x, xb = _combine_call(x1, gcol, ys, ln2_g[l], ln2_b[l])
        else:
            y_p, y_s = _combine_call(x1, gcol, ys, ln2_g[l], ln2_b[l], n_p=n_p)

        for k, v in zip(("pc", "pl", "pC", "pn", "pm", "sc", "sl", "sC", "sn", "sm"),
                        (pc, plru, pC, pn, pm, sc, slru, sC, sn, sm)):
            outs[k].append(v)

    y_prompt = y_p.reshape(bp, tp, d)
    y_sample = y_s.reshape(bs, SLOT, d)[:, :ts]
    st = {k: jnp.stack(v) for k, v in outs.items()}
    return (y_prompt, y_sample, st["pc"], st["pl"], st["pC"], st["pn"], st["pm"],
            st["sc"], st["sl"], st["sC"], st["sn"], st["sm"])
```

```python
import functools

import jax
import jax.numpy as jnp
from jax import lax
from jax.experimental import pallas as pl
from jax.experimental.pallas import tpu as pltpu

DEPTH = 4
D_MODEL = 2048
D_RG = 1024
RG_BLOCKS = 8
RG_BW = D_RG // RG_BLOCKS
CONV_W = 4
LRU_C = 8.0
D_ML = 1024
ML_HEADS = 4
ML_DH = D_ML // ML_HEADS
CHUNK = 64
N_EXPERTS = 16
N_GROUPS = 4
EXPERTS_PER_GROUP = N_EXPERTS // N_GROUPS
TOP_K = 2
D_FF = D_MODEL // 2
ALPHA = (2 * DEPTH) ** 0.25
LN_EPS = 1e-5
RMS_EPS = 1e-6
N_PROJ = 6
GATE_OFF = N_PROJ * D_RG

LANES = 128
SUBLANES = 8
MXU_DTYPE = jnp.bfloat16
SLOT = 128
TM_PROJ = 1024
TM_LN = 512
TM_OUT = 512
TM_CMB = 512
TB_MOE = 256
T_RG = 512
L_ML = 256
VMEM_LIMIT = 56 * 1024 * 1024


def _cparams(sem):
    return pltpu.CompilerParams(dimension_semantics=sem, vmem_limit_bytes=VMEM_LIMIT)


def _layer_norm(h, g, b):
    mu = jnp.mean(h, axis=-1, keepdims=True)
    hc = h - mu
    var = jnp.mean(hc * hc, axis=-1, keepdims=True)
    return hc * lax.rsqrt(var + LN_EPS) * g + b


def _emit_rows(y, wg_ref, bg_ref, o_ref, ob_ref, og_ref):
    yb = y.astype(ob_ref.dtype)
    o_ref[...] = y
    ob_ref[...] = yb
    og_ref[...] = jnp.dot(yb, wg_ref[...], preferred_element_type=jnp.float32) + bg_ref[...]


def _gate_specs(layer):
    return ([pl.BlockSpec((None, D_MODEL, LANES), lambda i: (layer, 0, 0)),
             pl.BlockSpec((None, 1, LANES), lambda i: (layer, 0, 0))],
            lambda tm: pl.BlockSpec((tm, LANES), lambda i: (i, 0)))


def _ln_kernel(xp_ref, xs_ref, g_ref, b_ref, wg_ref, bg_ref, o_ref, ob_ref, og_ref, *, n_prompt_tiles):
    x = jnp.where(pl.program_id(0) < n_prompt_tiles, xp_ref[...], xs_ref[...])
    _emit_rows(_layer_norm(x, g_ref[...], b_ref[...]), wg_ref, bg_ref, o_ref, ob_ref, og_ref)


def _ln_call(xp, xs, g, b, wg, bg):
    (n_p, d), n_s = xp.shape, xs.shape[0]
    tm = TM_LN
    npt = n_p // tm
    n = n_p + n_s
    row = pl.BlockSpec((tm, d), lambda i: (i, 0))
    vec = pl.BlockSpec((1, d), lambda i: (0, 0))
    gate_in, gate_out = _gate_specs(0)
    return pl.pallas_call(
        functools.partial(_ln_kernel, n_prompt_tiles=npt),
        grid=(n // tm,),
        in_specs=[pl.BlockSpec((tm, d), lambda i: (jnp.minimum(i, npt - 1), 0)),
                  pl.BlockSpec((tm, d), lambda i: (jnp.maximum(i - npt, 0), 0)),
                  vec, vec] + gate_in,
        out_specs=[row, row, gate_out(tm)],
        out_shape=[jax.ShapeDtypeStruct((n, d), jnp.float32),
                   jax.ShapeDtypeStruct((n, d), MXU_DTYPE),
                   jax.ShapeDtypeStruct((n, LANES), jnp.float32)],
        compiler_params=_cparams(("parallel",)),
        name="ln_in",
    )(xp, xs, g.reshape(1, d), b.reshape(1, d), wg, bg)


def _proj_kernel(x_ref, w_ref, b_ref, o_ref):
    o_ref[...] = jnp.dot(x_ref[...], w_ref[...], preferred_element_type=jnp.float32) + b_ref[...]


def _in_proj_call(xb, w_in_m, b_in, layer):
    n, d = xb.shape
    depth, d_in = b_in.shape
    return pl.pallas_call(
        _proj_kernel,
        grid=(N_PROJ, n // TM_PROJ),
        in_specs=[pl.BlockSpec((TM_PROJ, d), lambda j, i: (i, 0)),
                  pl.BlockSpec((None, d, D_RG), lambda j, i: (layer, 0, j)),
                  pl.BlockSpec((None, 1, D_RG), lambda j, i: (layer, 0, j))],
        out_specs=pl.BlockSpec((None, TM_PROJ, D_RG), lambda j, i: (j, i, 0)),
        out_shape=jax.ShapeDtypeStruct((N_PROJ, n, D_RG), jnp.float32),
        compiler_params=_cparams(("parallel", "parallel")),
        name="in_proj",
    )(xb, w_in_m, b_in.reshape(depth, 1, d_in))


def _rglru_kernel(xr_ref, gr_ref, cs_ref, h0_ref, cw_ref, cb_ref, wg_ref, ba_ref, bx_ref, lam_ref,
                  mix_ref, cso_ref, ho_ref, ext_ref, a_ref, u_ref, hc_ref, *, tv, n_t):
    t = pl.program_id(1)
    tt = mix_ref.shape[0]

    @pl.when(t == 0)
    def _():
        ext_ref[0:SUBLANES, :] = jnp.zeros((SUBLANES, D_RG), jnp.float32)
        ext_ref[SUBLANES - (CONV_W - 1):SUBLANES, :] = cs_ref[0]
        hc_ref[...] = h0_ref[0]

    u_in = xr_ref[0:tv, :]
    ext_ref[SUBLANES:SUBLANES + tv, :] = u_in
    cw = cw_ref[...]
    xc = cb_ref[...]
    for tap in range(CONV_W - 1):
        off = SUBLANES - (CONV_W - 1) + tap
        xc = xc + ext_ref[off:off + tv, :] * cw[tap:tap + 1, :]
    xc = xc + u_in * cw[CONV_W - 1:CONV_W, :]

    @pl.when(t == n_t - 1)
    def _():
        cso_ref[0] = ext_ref[SUBLANES + tv - (CONV_W - 1):SUBLANES + tv, :]

    ext_ref[0:SUBLANES, :] = ext_ref[tv:tv + SUBLANES, :]

    xcb = xc.astype(MXU_DTYPE)
    rowmod = lax.broadcasted_iota(jnp.int32, (tv, RG_BW), 0) % SUBLANES
    for nb in range(RG_BLOCKS):
        cs = slice(nb * RG_BW, (nb + 1) * RG_BW)
        pre = jnp.dot(xcb[:, cs], wg_ref[nb], preferred_element_type=jnp.float32)
        r = jax.nn.sigmoid(pre[:, :RG_BW] + ba_ref[:, cs])
        ig = jax.nn.sigmoid(pre[:, RG_BW:] + bx_ref[:, cs])
        log_a = LRU_C * r * jax.nn.log_sigmoid(lam_ref[:, cs])
        a = jnp.exp(log_a)
        th = jnp.tanh(log_a)
        u = jnp.sqrt(-2.0 * th / (1.0 - th)) * (ig * xc[:, cs])
        for s in (1, 2, 4):
            a_s = pltpu.roll(a, s, 0)
            u_s = pltpu.roll(u, s, 0)
            m = rowmod >= s
            u = jnp.where(m, a * u_s + u, u)
            a = jnp.where(m, a * a_s, a)
        a_ref[0:tv, cs] = a
        u_ref[0:tv, cs] = u

    def body(g, carry):
        rows = pl.ds(pl.multiple_of(g * SUBLANES, SUBLANES), SUBLANES)
        h = a_ref[rows, :] * carry + u_ref[rows, :]
        u_ref[rows, :] = h
        return h[SUBLANES - 1:SUBLANES, :]

    h_last = lax.fori_loop(0, tv // SUBLANES, body, hc_ref[...])
    hc_ref[...] = h_last
    ho_ref[0] = h_last

    mix_ref[0:tv, :] = (jax.nn.gelu(gr_ref[0:tv, :]) * u_ref[0:tv, :]).astype(mix_ref.dtype)
    if tv < tt:
        mix_ref[tv:tt, :] = jnp.zeros((tt - tv, D_RG), mix_ref.dtype)


def _rglru_call(z3, conv0, h0, cw, cb, wg, ba, bx, lam, *, row0, seq_stride, tt, tv, n_t, name):
    n_seq = conv0.shape[0]
    blk0, sb = row0 // tt, seq_stride // tt

    def rows(b, t):
        return blk0 + b * sb + t

    vec = pl.BlockSpec((1, D_RG), lambda b, t: (0, 0))
    in_specs = [
        pl.BlockSpec((None, tt, D_RG), lambda b, t: (0, rows(b, t), 0)),
        pl.BlockSpec((None, tt, D_RG), lambda b, t: (1, rows(b, t), 0)),
        pl.BlockSpec((1, CONV_W - 1, D_RG), lambda b, t: (b, 0, 0)),
        pl.BlockSpec((1, 1, D_RG), lambda b, t: (b, 0, 0)),
        pl.BlockSpec((CONV_W, D_RG), lambda b, t: (0, 0)),
        vec,
        pl.BlockSpec((RG_BLOCKS, RG_BW, 2 * RG_BW), lambda b, t: (0, 0, 0)),
        vec, vec, vec,
    ]
    args = [z3, z3, conv0, h0.reshape(n_seq, 1, D_RG), cw, cb.reshape(1, D_RG), wg,
            ba.reshape(1, D_RG), bx.reshape(1, D_RG), lam.reshape(1, D_RG)]
    mix, conv_out, h_out = pl.pallas_call(
        functools.partial(_rglru_kernel, tv=tv, n_t=n_t),
        grid=(n_seq, n_t),
        in_specs=in_specs,
        out_specs=[pl.BlockSpec((tt, D_RG), lambda b, t: (b * sb + t, 0)),
                   pl.BlockSpec((1, CONV_W - 1, D_RG), lambda b, t: (b, 0, 0)),
                   pl.BlockSpec((1, 1, D_RG), lambda b, t: (b, 0, 0))],
        out_shape=[jax.ShapeDtypeStruct((n_seq * seq_stride, D_RG), MXU_DTYPE),
                   jax.ShapeDtypeStruct((n_seq, CONV_W - 1, D_RG), jnp.float32),
                   jax.ShapeDtypeStruct((n_seq, 1, D_RG), jnp.float32)],
        scratch_shapes=[pltpu.VMEM((tt + SUBLANES, D_RG), jnp.float32),
                        pltpu.VMEM((tt, D_RG), jnp.float32),
                        pltpu.VMEM((tt, D_RG), jnp.float32),
                        pltpu.VMEM((1, D_RG), jnp.float32)],
        compiler_params=_cparams(("parallel", "arbitrary")),
        name=name,
    )(*args)
    return mix, conv_out, h_out.reshape(n_seq, D_RG)


def _mlstm_kernel(q_ref, k_ref, v_ref, o_ref, g_ref, c0_ref, n0_ref, m0_ref, ng_ref,
                  mix_ref, co_ref, no_ref, mo_ref, c_s, n_s, m_s, *, lc_last, tv, n_c):
    c = pl.program_id(1)
    ll = mix_ref.shape[0]
    h_n = ML_HEADS

    @pl.when(c == 0)
    def _():
        c_s[...] = c0_ref[0]
        n_s[...] = n0_ref[0]
        m_s[...] = m0_ref[0]

    dn_t = (((1,), (1,)), ((), ()))
    dn_c0 = (((0,), (0,)), ((), ()))

    def run(lc):
        g = g_ref[...]
        row = lax.broadcasted_iota(jnp.int32, (ll, LANES), 0)
        lf = jax.nn.log_sigmoid(g)
        if tv < ll:
            lane = lax.broadcasted_iota(jnp.int32, (ll, LANES), 1)
            g = jnp.where((row >= tv) & (lane < h_n), -jnp.inf, g)
            lf = jnp.where(row >= tv, 0.0, lf)
        rowc = row % lc
        bcum = lf
        s = 1
        while s < lc:
            bcum = bcum + jnp.where(rowc >= s, pltpu.roll(bcum, s, 0), 0.0)
            s *= 2
        g_t = g.T
        b_t = bcum.T
        ti = lax.broadcasted_iota(jnp.int32, (lc, lc), 0)
        si = lax.broadcasted_iota(jnp.int32, (lc, lc), 1)
        causal = si <= ti

        m_val = [m_s[:, h:h + 1] for h in range(h_n)]
        n_val = [n_s[h:h + 1, :] for h in range(h_n)]
        for j in range(ll // lc):
            rs = slice(j * lc, (j + 1) * lc)
            for h in range(h_n):
                cs = slice(h * ML_DH, (h + 1) * ML_DH)
                qf = q_ref[rs, cs]
                kf = k_ref[rs, cs] * (ML_DH ** -0.5)
                vf = v_ref[rs, cs]
                qb, kb, vb = qf.astype(MXU_DTYPE), kf.astype(MXU_DTYPE), vf.astype(MXU_DTYPE)
                ig_col = g[rs, h:h + 1]
                b_col = bcum[rs, h_n + h:h_n + h + 1]
                ig_row = g_t[h:h + 1, rs]
                b_row = b_t[h_n + h:h_n + h + 1, rs]
                m_prev = m_val[h]

                logw = jnp.where(causal, b_col - b_row + ig_row, -jnp.inf)
                gq = b_col + m_prev
                mt = jnp.maximum(gq, jnp.max(logw, axis=-1, keepdims=True))
                w_intra = jnp.exp(logw - mt)
                w_state = jnp.exp(gq - mt)
                s_mat = lax.dot_general(qb, kb, dn_t, preferred_element_type=jnp.float32) * w_intra
                c_prev = c_s[h]
                n_prev = n_val[h]
                num = (w_state * jnp.dot(qb, c_prev.astype(MXU_DTYPE), preferred_element_type=jnp.float32)
                       + jnp.dot(s_mat.astype(MXU_DTYPE), vb, preferred_element_type=jnp.float32))
                den = (w_state * jnp.sum(qf * n_prev, axis=-1, keepdims=True)
                       + jnp.sum(s_mat, axis=-1, keepdims=True))
                hh = num / jnp.maximum(jnp.abs(den), jnp.exp(-mt))

                b_last = b_col[lc - 1:lc, :]
                logu_row = b_last - b_row + ig_row
                m_new = jnp.maximum(b_last + m_prev, jnp.max(logu_row, axis=-1, keepdims=True))
                u_col = jnp.exp(b_last - b_col + ig_col - m_new)
                decay = jnp.exp(b_last + m_prev - m_new)
                ku = kf * u_col
                c_s[h] = decay * c_prev + lax.dot_general(ku.astype(MXU_DTYPE), vb, dn_c0,
                                                          preferred_element_type=jnp.float32)
                n_val[h] = decay * n_prev + jnp.sum(ku, axis=0, keepdims=True)
                m_val[h] = m_new

                hm = hh * lax.rsqrt(jnp.mean(hh * hh, axis=-1, keepdims=True) + RMS_EPS) * ng_ref[:, cs]
                mix_ref[rs, cs] = (hm * jax.nn.sigmoid(o_ref[rs, cs])).astype(mix_ref.dtype)

        for h in range(h_n):
            m_s[:, h:h + 1] = m_val[h]
            n_s[h:h + 1, :] = n_val[h]

    if lc_last == ll:
        run(ll)
    else:
        @pl.when(c < n_c - 1)
        def _():
            run(ll)

        @pl.when(c == n_c - 1)
        def _():
            run(lc_last)

    @pl.when(c == n_c - 1)
    def _():
        co_ref[0] = c_s[...]
        no_ref[0] = n_s[...]
        mo_ref[0] = m_s[...]


def _mlstm_call(z3, gates, c0, n0, m0, ng, *, row0, seq_stride, ll, lc_last, tv, n_c, name):
    n_seq = c0.shape[0]
    blk0, sb = row0 // ll, seq_stride // ll

    def rows(b, c):
        return blk0 + b * sb + c

    def zspec(j):
        return pl.BlockSpec((None, ll, D_ML), lambda b, c: (j, rows(b, c), 0))

    in_specs = [
        zspec(2), zspec(3), zspec(4), zspec(5),
        pl.BlockSpec((ll, LANES), lambda b, c: (rows(b, c), 0)),
        pl.BlockSpec((1, ML_HEADS, ML_DH, ML_DH), lambda b, c: (b, 0, 0, 0)),
        pl.BlockSpec((1, ML_HEADS, ML_DH), lambda b, c: (b, 0, 0)),
        pl.BlockSpec((1, 1, ML_HEADS), lambda b, c: (b, 0, 0)),
        pl.BlockSpec((1, D_ML), lambda b, c: (0, 0)),
    ]
    args = [z3, z3, z3, z3, gates, c0, n0, m0.reshape(n_seq, 1, ML_HEADS), ng.reshape(1, D_ML)]
    mix, c_out, n_out, m_out = pl.pallas_call(
        functools.partial(_mlstm_kernel, lc_last=lc_last, tv=tv, n_c=n_c),
        grid=(n_seq, n_c),
        in_specs=in_specs,
        out_specs=[pl.BlockSpec((ll, D_ML), lambda b, c: (b * sb + c, 0)),
                   pl.BlockSpec((1, ML_HEADS, ML_DH, ML_DH), lambda b, c: (b, 0, 0, 0)),
                   pl.BlockSpec((1, ML_HEADS, ML_DH), lambda b, c: (b, 0, 0)),
                   pl.BlockSpec((1, 1, ML_HEADS), lambda b, c: (b, 0, 0))],
        out_shape=[jax.ShapeDtypeStruct((n_seq * seq_stride, D_ML), MXU_DTYPE),
                   jax.ShapeDtypeStruct((n_seq, ML_HEADS, ML_DH, ML_DH), jnp.float32),
                   jax.ShapeDtypeStruct((n_seq, ML_HEADS, ML_DH), jnp.float32),
                   jax.ShapeDtypeStruct((n_seq, 1, ML_HEADS), jnp.float32)],
        scratch_shapes=[pltpu.VMEM((ML_HEADS, ML_DH, ML_DH), jnp.float32),
                        pltpu.VMEM((ML_HEADS, ML_DH), jnp.float32),
                        pltpu.VMEM((1, ML_HEADS), jnp.float32)],
        compiler_params=_cparams(("parallel", "arbitrary")),
        name=name,
    )(*args)
    return mix, c_out, n_out, m_out.reshape(n_seq, ML_HEADS)


def _first_argmax(vals):
    best, idx = vals[0], jnp.zeros(vals[0].shape, jnp.int32)
    for j in range(1, len(vals)):
        better = vals[j] > best
        best = jnp.where(better, vals[j], best)
        idx = jnp.where(better, j, idx)
    return best, idx


def _pick(vals, idx):
    out = vals[0]
    for j in range(1, len(vals)):
        out = jnp.where(idx == j, vals[j], out)
    return out


def _outproj_kernel(x_ref, rgp_ref, mlp_ref, rgs_ref, mls_ref, wo_ref, g_ref, b_ref, rwt_ref, rb_ref,
                    x1_ref, x1b_ref, e_ref, gc_ref, *, n_prompt_tiles):
    tm = x_ref.shape[0]
    is_prompt = pl.program_id(0) < n_prompt_tiles
    mix_rg = jnp.where(is_prompt, rgp_ref[...], rgs_ref[...])
    mix_ml = jnp.where(is_prompt, mlp_ref[...], mls_ref[...])
    y = (jnp.dot(mix_rg, wo_ref[0:D_RG, :], preferred_element_type=jnp.float32)
         + jnp.dot(mix_ml, wo_ref[D_RG:, :], preferred_element_type=jnp.float32))
    x1 = _layer_norm(ALPHA * x_ref[...] + y, g_ref[...], b_ref[...])
    x1_ref[...] = x1
    x1b = x1.astype(x1b_ref.dtype)
    x1b_ref[...] = x1b

    logits = lax.dot_general(rwt_ref[...], x1b, (((1,), (1,)), ((), ())),
                             preferred_element_type=jnp.float32)
    ex = jnp.exp(logits - jnp.max(logits, axis=0, keepdims=True))
    probs = ex / jnp.sum(ex, axis=0, keepdims=True)
    sel = probs + rb_ref[...]
    sel_r = [sel[e:e + 1, :] for e in range(N_EXPERTS)]
    prob_r = [probs[e:e + 1, :] for e in range(N_EXPERTS)]
    scores = []
    for gi in range(N_GROUPS):
        v = sel_r[gi * EXPERTS_PER_GROUP:(gi + 1) * EXPERTS_PER_GROUP]
        top2 = None
        for i in range(EXPERTS_PER_GROUP):
            for j in range(i + 1, EXPERTS_PER_GROUP):
                pair = v[i] + v[j]
                top2 = pair if top2 is None else jnp.maximum(top2, pair)
        scores.append(top2)
    _, g_idx = _first_argmax(scores)
    in_sel = [_pick([sel_r[gi * EXPERTS_PER_GROUP + j] for gi in range(N_GROUPS)], g_idx)
              for j in range(EXPERTS_PER_GROUP)]
    in_prob = [_pick([prob_r[gi * EXPERTS_PER_GROUP + j] for gi in range(N_GROUPS)], g_idx)
               for j in range(EXPERTS_PER_GROUP)]
    _, loc0 = _first_argmax(in_sel)
    masked = [jnp.where(loc0 == j, -jnp.inf, in_sel[j]) for j in range(EXPERTS_PER_GROUP)]
    _, loc1 = _first_argmax(masked)
    p0, p1 = _pick(in_prob, loc0), _pick(in_prob, loc1)
    psum = p0 + p1
    e0 = g_idx * EXPERTS_PER_GROUP + loc0
    e1 = g_idx * EXPERTS_PER_GROUP + loc1
    e_ref[...] = jnp.concatenate([e0, e1, jnp.zeros((SUBLANES - TOP_K, tm), jnp.int32)], axis=0)
    gates_t = jnp.concatenate([p0 / psum, p1 / psum, jnp.zeros((LANES - TOP_K, tm), jnp.float32)], axis=0)
    gc_ref[...] = gates_t.T


def _outproj_call(x, rg_p, ml_p, rg_s, ml_s, wo, g, b, rwt, rb):
    n, d = x.shape
    tm = TM_OUT
    npt = rg_p.shape[0] // tm
    row = pl.BlockSpec((tm, d), lambda i: (i, 0))
    half_p = pl.BlockSpec((tm, D_RG), lambda i: (jnp.minimum(i, npt - 1), 0))
    half_s = pl.BlockSpec((tm, D_RG), lambda i: (jnp.maximum(i - npt, 0), 0))
    vec = pl.BlockSpec((1, d), lambda i: (0, 0))
    return pl.pallas_call(
        functools.partial(_outproj_kernel, n_prompt_tiles=npt),
        grid=(n // tm,),
        in_specs=[row, half_p, half_p, half_s, half_s,
                  pl.BlockSpec((d, d), lambda i: (0, 0), pipeline_mode=pl.Buffered(1)),
                  vec, vec,
                  pl.BlockSpec((N_EXPERTS, d), lambda i: (0, 0)),
                  pl.BlockSpec((N_EXPERTS, 1), lambda i: (0, 0))],
        out_specs=[row, row,
                   pl.BlockSpec((SUBLANES, tm), lambda i: (0, i)),
                   pl.BlockSpec((tm, LANES), lambda i: (i, 0))],
        out_shape=[jax.ShapeDtypeStruct((n, d), jnp.float32),
                   jax.ShapeDtypeStruct((n, d), MXU_DTYPE),
                   jax.ShapeDtypeStruct((SUBLANES, n), jnp.int32),
                   jax.ShapeDtypeStruct((n, LANES), jnp.float32)],
        compiler_params=_cparams(("parallel",)),
        name="outproj_ln_router",
    )(x, rg_p, ml_p, rg_s, ml_s, wo, g.reshape(1, d), b.reshape(1, d), rwt, rb.reshape(N_EXPERTS, 1))


def _dispatch_plan(e_t, tb, nb):
    n = e_t.shape[1]
    p = TOP_K * n
    i32 = jnp.int32
    flat_e = e_t.T.reshape(p)
    sk = jnp.sort(flat_e * p + jnp.arange(p, dtype=i32))
    order = sk % p
    counts = jnp.sum((flat_e[:, None] == jnp.arange(N_EXPERTS, dtype=i32)[None, :]).astype(i32), axis=0)
    start = jnp.cumsum(counts) - counts
    pcounts = (counts + tb - 1) // tb * tb
    pend = jnp.cumsum(pcounts)
    pstart = pend - pcounts
    r = jnp.arange(nb * tb, dtype=i32)
    er = jnp.minimum(jnp.sum((r[:, None] >= pend[None, :]).astype(i32), axis=1), N_EXPERTS - 1)
    off = r - pstart[er]
    real = (r < pend[-1]) & (off < counts[er])
    n_real_before = jnp.where(r < pend[-1], start[er] + jnp.minimum(off, counts[er]), p)
    pair = order[jnp.clip(start[er] + off, 0, p - 1)]
    tok, k = pair // TOP_K, pair % TOP_K
    src_tok = jnp.where(real, tok, 0).astype(i32)
    dst_row = jnp.where(real, k * n + tok, p + (r - n_real_before)).astype(i32)
    blk_e = er[::tb]
    return src_tok, dst_row, blk_e


def _moe_kernel(be_ref, src_ref, srcn_ref, dst_ref, dstp_ref, x_hbm, w1_ref, w3_ref, w2_ref, y_hbm,
                xbuf, ybuf, gsem, ssem):
    r = pl.program_id(0)
    n_r = pl.num_programs(0)
    tb = ybuf.shape[1]
    slot = r % 2
    other = 1 - slot

    def gather_row(idx_ref, s, j):
        pltpu.make_async_copy(x_hbm.at[pl.ds(idx_ref[j], 1), :],
                              xbuf.at[s, pl.ds(j, 1), :], gsem.at[s]).start()

    def scatter_row(idx_ref, s, j):
        pltpu.make_async_copy(ybuf.at[s, pl.ds(j, 1), :],
                              y_hbm.at[pl.ds(idx_ref[j], 1), :], ssem.at[s]).start()

    def wait_gather(s):
        pltpu.make_async_copy(x_hbm.at[pl.ds(0, tb), :], xbuf.at[s], gsem.at[s]).wait()

    def wait_scatter(s):
        pltpu.make_async_copy(ybuf.at[s], y_hbm.at[pl.ds(0, tb), :], ssem.at[s]).wait()

    def expert(s):
        xb = xbuf[s].astype(MXU_DTYPE)
        h1 = jnp.dot(xb, w1_ref[...], preferred_element_type=jnp.float32)
        h3 = jnp.dot(xb, w3_ref[...], preferred_element_type=jnp.float32)
        hb = (jax.nn.silu(h1) * h3).astype(MXU_DTYPE)
        ybuf[s] = jnp.dot(hb, w2_ref[...], preferred_element_type=jnp.float32)

    @pl.when(r == 0)
    def _():
        for j in range(tb):
            gather_row(src_ref, 0, j)
        for j in range(tb):
            gather_row(srcn_ref, 1, j)
        wait_gather(0)
        expert(0)

    @pl.when(r > 0)
    def _():
        for j in range(tb):
            gather_row(srcn_ref, other, j)
        wait_gather(slot)
        for j in range(tb):
            scatter_row(dstp_ref, other, j)
        expert(slot)
        wait_scatter(other)

    @pl.when(r == n_r - 1)
    def _():
        wait_gather(other)

        def put(j, carry):
            scatter_row(dst_ref, slot, j)
            return carry
        lax.fori_loop(0, tb, put, 0)
        wait_scatter(slot)


def _moe_call(x1, src_tok, dst_row, blk_e, w1, w3, w2, layer, tb, nb):
    n, d = x1.shape
    f = w1.shape[-1]
    assert nb >= 2

    def wblk(r, be):
        return (layer, be[r], 0, 0)

    grid_spec = pltpu.PrefetchScalarGridSpec(
        num_scalar_prefetch=1,
        grid=(nb,),
        in_specs=[
            pl.BlockSpec((tb,), lambda r, be: (r,), memory_space=pltpu.SMEM),
            pl.BlockSpec((tb,), lambda r, be: (jnp.minimum(r + 1, nb - 1),), memory_space=pltpu.SMEM),
            pl.BlockSpec((tb,), lambda r, be: (r,), memory_space=pltpu.SMEM),
            pl.BlockSpec((tb,), lambda r, be: (jnp.maximum(r - 1, 0),), memory_space=pltpu.SMEM),
            pl.BlockSpec(memory_space=pl.ANY),
            pl.BlockSpec((None, None, d, f), wblk),
            pl.BlockSpec((None, None, d, f), wblk),
            pl.BlockSpec((None, None, f, d), wblk),
        ],
        out_specs=pl.BlockSpec(memory_space=pl.ANY),
        scratch_shapes=[pltpu.VMEM((2, tb, d), jnp.float32), pltpu.VMEM((2, tb, d), jnp.float32),
                        pltpu.SemaphoreType.DMA((2,)), pltpu.SemaphoreType.DMA((2,))],
    )
    return pl.pallas_call(
        _moe_kernel,
        grid_spec=grid_spec,
        out_shape=jax.ShapeDtypeStruct((nb * tb, d), jnp.float32),
        compiler_params=_cparams(("arbitrary",)),
        name="moe_experts",
    )(blk_e, src_tok, src_tok, dst_row, dst_row, x1, w1, w3, w2)


def _combine_math(x_ref, y0_ref, y1_ref, gc_ref, g_ref, b_ref):
    gc = gc_ref[...]
    f = y0_ref[...] * gc[:, 0:1] + y1_ref[...] * gc[:, 1:2]
    return _layer_norm(ALPHA * x_ref[...] + f, g_ref[...], b_ref[...])


def _combine_kernel(x_ref, y0_ref, y1_ref, gc_ref, g_ref, b_ref, wg_ref, bg_ref, o_ref, ob_ref, og_ref):
    y = _combine_math(x_ref, y0_ref, y1_ref, gc_ref, g_ref, b_ref)
    _emit_rows(y, wg_ref, bg_ref, o_ref, ob_ref, og_ref)


def _combine_last_kernel(x_ref, y0_ref, y1_ref, gc_ref, g_ref, b_ref, op_ref, os_ref, *, n_prompt_tiles):
    y = _combine_math(x_ref, y0_ref, y1_ref, gc_ref, g_ref, b_ref)
    is_prompt = pl.program_id(0) < n_prompt_tiles

    @pl.when(is_prompt)
    def _():
        op_ref[...] = y

    @pl.when(jnp.logical_not(is_prompt))
    def _():
        os_ref[...] = y


def _combine_call(x1, gcol, ys, g, b, wg=None, bg=None, next_layer=None, n_p=None):
    n, d = x1.shape
    tm = TM_CMB
    nt = n // tm
    row = pl.BlockSpec((tm, d), lambda i: (i, 0))
    vec = pl.BlockSpec((1, d), lambda i: (0, 0))
    in_specs = [row, row, pl.BlockSpec((tm, d), lambda i: (nt + i, 0)),
                pl.BlockSpec((tm, LANES), lambda i: (i, 0)), vec, vec]
    args = (x1, ys, ys, gcol, g.reshape(1, d), b.reshape(1, d))
    if n_p is None:
        gate_in, gate_out = _gate_specs(next_layer)
        return pl.pallas_call(
            _combine_kernel,
            grid=(nt,),
            in_specs=in_specs + gate_in,
            out_specs=[row, row, gate_out(tm)],
            out_shape=[jax.ShapeDtypeStruct((n, d), jnp.float32),
                       jax.ShapeDtypeStruct((n, d), MXU_DTYPE),
                       jax.ShapeDtypeStruct((n, LANES), jnp.float32)],
            compiler_params=_cparams(("parallel",)),
            name="combine_ln",
        )(*args, wg, bg)
    npt = n_p // tm
    return pl.pallas_call(
        functools.partial(_combine_last_kernel, n_prompt_tiles=npt),
        grid=(nt,),
        in_specs=in_specs,
        out_specs=[pl.BlockSpec((tm, d), lambda i: (jnp.minimum(i, npt - 1), 0)),
                   pl.BlockSpec((tm, d), lambda i: (jnp.maximum(i - npt, 0), 0))],
        out_shape=[jax.ShapeDtypeStruct((n_p, d), jnp.float32),
                   jax.ShapeDtypeStruct((n - n_p, d), jnp.float32)],
        compiler_params=_cparams(("arbitrary",)),
        name="combine_ln_last",
    )(*args)


def kernel(x_prompt, x_sample, state_conv, state_lru, state_mlstm_C, state_mlstm_n, state_mlstm_m,
           ln_in_g, ln_in_b, w_in, b_in, conv_w, conv_b, lru_wa, lru_ba, lru_wx, lru_bx, lru_lambda,
           mlstm_norm_g, w_out, ln1_g, ln1_b, router_w, router_b, w1, w3, w2, ln2_g, ln2_b):
    bp, tp, d = x_prompt.shape
    bs, ts, _ = x_sample.shape
    depth = w_in.shape[0]
    n_p = bp * tp
    n_s = bs * SLOT
    n_pad = n_p + n_s
    assert d == D_MODEL and ts <= min(SLOT, CHUNK) and ts % SUBLANES == 0
    assert tp % T_RG == 0 and tp % L_ML == 0
    assert n_p % TM_PROJ == 0 and n_s % TM_PROJ == 0
    f32 = jnp.float32

    nb = -(-(TOP_K * n_pad + N_EXPERTS * (TB_MOE - 1)) // TB_MOE)

    w_in_m = w_in.astype(MXU_DTYPE)
    w_gate_m = jnp.pad(w_in_m[:, :, GATE_OFF:], ((0, 0), (0, 0), (0, LANES - 2 * ML_HEADS)))
    b_gate = jnp.pad(b_in[:, GATE_OFF:], ((0, 0), (0, LANES - 2 * ML_HEADS))).reshape(depth, 1, LANES)
    w_rg_m = jnp.concatenate([lru_wa, lru_wx], axis=-1).astype(MXU_DTYPE)
    w_out_m = w_out.astype(MXU_DTYPE)
    rwt_m = router_w.T.astype(MXU_DTYPE)
    w1_m, w3_m, w2_m = w1.astype(MXU_DTYPE), w3.astype(MXU_DTYPE), w2.astype(MXU_DTYPE)

    zp = lambda *s: jnp.zeros(s, f32)
    x, xb, gates = _ln_call(x_prompt.reshape(n_p, d),
                            jnp.pad(x_sample, ((0, 0), (0, SLOT - ts), (0, 0))).reshape(n_s, d),
                            ln_in_g, ln_in_b, w_gate_m, b_gate)
    outs = {k: [] for k in ("pc", "pl", "pC", "pn", "pm", "sc", "sl", "sC", "sn", "sm")}
    for l in range(depth):
        z3 = _in_proj_call(xb, w_in_m, b_in, l)

        rg_w = (conv_w[l], conv_b[l], w_rg_m[l], lru_ba[l], lru_bx[l], lru_lambda[l])
        rg_p, pc, plru = _rglru_call(z3, zp(bp, CONV_W - 1, D_RG), zp(bp, D_RG), *rg_w, row0=0, seq_stride=tp,
                                     tt=T_RG, tv=T_RG, n_t=tp // T_RG, name="rglru_prompt")
        rg_s, sc, slru = _rglru_call(z3, state_conv[l], state_lru[l], *rg_w, row0=n_p, seq_stride=SLOT,
                                     tt=SLOT, tv=ts, n_t=1, name="rglru_sample")
        ml_p, pC, pn, pm = _mlstm_call(z3, gates, zp(bp, ML_HEADS, ML_DH, ML_DH), zp(bp, ML_HEADS, ML_DH),
                                       zp(bp, ML_HEADS), mlstm_norm_g[l], row0=0, seq_stride=tp,
                                       ll=L_ML, lc_last=min(CHUNK, tp), tv=L_ML, n_c=tp // L_ML, name="mlstm_prompt")
        ml_s, sC, sn, sm = _mlstm_call(z3, gates, state_mlstm_C[l], state_mlstm_n[l], state_mlstm_m[l],
                                       mlstm_norm_g[l], row0=n_p, seq_stride=SLOT,
                                       ll=SLOT, lc_last=SLOT, tv=ts, n_c=1, name="mlstm_sample")

        x1, x1b, e_t, gcol = _outproj_call(x, rg_p, ml_p, rg_s, ml_s, w_out_m[l], ln1_g[l], ln1_b[l],
                                           rwt_m, router_b)
        src_tok, dst_row, blk_e = _dispatch_plan(e_t[:TOP_K], TB_MOE, nb)
        ys = _moe_call(x1, src_tok, dst_row, blk_e, w1_m, w3_m, w2_m, l, TB_MOE, nb)
        if l + 1 < depth:
            x, xb, gates = _combine_call(x1, gcol, ys, ln2_g[l], ln2_b[l], w_gate_m, b_gate, next_layer=l + 1)
        else:
            y_p, y_s = _combine_call(x1, gcol, ys, ln2_g[l], ln2_b[l], n_p=n_p)

        for k, v in zip(("pc", "pl", "pC", "pn", "pm", "sc", "sl", "sC", "sn", "sm"),
                        (pc, plru, pC, pn, pm, sc, slru, sC, sn, sm)):
            outs[k].append(v)

    y_prompt = y_p.reshape(bp, tp, d)
    y_sample = y_s.reshape(bs, SLOT, d)[:, :ts]
    st = {k: jnp.stack(v) for k, v in outs.items()}
    return (y_prompt, y_sample, st["pc"], st["pl"], st["pC"], st["pn"], st["pm"],
            st["sc"], st["sl"], st["sC"], st["sn"], st["sm"])
```
